```python
import jax
import jax.numpy as jnp
from jax import lax
import numpy as np

D_MODEL = 2048
BATCH = 16
SEQ = 2048
DEPTH = 4

GRID_W = 64
CTX_LEN = 256
N_MIXERS = 3
N_MOD = 6
RMS_EPS = 1e-6
POOL_WINDOWS = (2, 4, 8, 16)
POOL_GROUPS = len(POOL_WINDOWS)
POOL_GROUP_DIM = D_MODEL // POOL_GROUPS
NA_HEAD_DIM = 32
NA_HEADS = D_MODEL // NA_HEAD_DIM
NA_ROWS = 8
NA_COLS = 16
FNET_GROUPS = 4
FNET_GROUP_DIM = D_MODEL // FNET_GROUPS
N_EXPERTS = 32
TOP_K = 4
D_EXPERT = 3 * D_MODEL // 8
SWIGLU_LIMIT = 7.0
SWIGLU_ALPHA = 1.702
MOE_BLOCK = 256

kernel_name = 'hybrid_pool_natten_fnet_moe_dit'


def n_layers_of_kind(kind):
    return (DEPTH + N_MIXERS - 1 - kind) // N_MIXERS


def rms_norm(x, gain):
    xf = x.astype(jnp.float32)
    y = xf * lax.rsqrt(jnp.mean(xf * xf, axis=-1, keepdims=True) + RMS_EPS)
    return (y * gain.astype(jnp.float32)).astype(x.dtype)


def modulate(h, shift, scale):
    return h * (1 + scale) + shift


def multiscale_pool(h, w_pool, scale):
    n = h.shape[1]
    hf = h.astype(jnp.float32)
    cs = jnp.concatenate([jnp.zeros_like(hf[:, :1]), jnp.cumsum(hf, axis=1)], axis=1)
    t = jnp.arange(n)
    outs = []
    for g, w in enumerate(POOL_WINDOWS):
        lo = jnp.clip(t - w // 2, 0, n)
        hi = jnp.clip(t + w - w // 2, 0, n)
        sl = slice(g * POOL_GROUP_DIM, (g + 1) * POOL_GROUP_DIM)
        csg = cs[:, :, sl]
        mean = (csg[:, hi] - csg[:, lo]) / (hi - lo).astype(jnp.float32)[:, None]
        outs.append((mean - hf[:, :, sl]).astype(h.dtype) @ w_pool[g])
    return jnp.concatenate(outs, axis=-1) * scale


def fourier_mix(h, w_f, b_f):
    bsz, n, d = h.shape
    hg = h.astype(jnp.float32).reshape(bsz, n, FNET_GROUPS, FNET_GROUP_DIM)
    f = jnp.fft.fft2(hg, axes=(1, 3), norm='ortho').real
    return f.reshape(bsz, n, d).astype(h.dtype) @ w_f + b_f


def neighbourhood_attention(hx, hc, w_qkv, b_qkv, rpb, w_o, b_o, ctx_queries):
    bsz, n_lat, d = hx.shape
    n_ctx = hc.shape[1]
    rows = n_lat // GRID_W
    kr = min(NA_ROWS, rows)
    scale = NA_HEAD_DIM ** -0.5
    qkv = (hx @ w_qkv + b_qkv).reshape(bsz, rows, GRID_W, 3, NA_HEADS, NA_HEAD_DIM)
    q, k, v = qkv[:, :, :, 0], qkv[:, :, :, 1], qkv[:, :, :, 2]
    kv_c = (hc @ w_qkv[:, d:] + b_qkv[d:]).reshape(bsz, n_ctx, 2, NA_HEADS, NA_HEAD_DIM)
    kc, vc = kv_c[:, :, 0], kv_c[:, :, 1]

    col = jnp.arange(GRID_W)
    c0 = jnp.clip(col - NA_COLS // 2, 0, GRID_W - NA_COLS)
    col_mask = (col[None, :] >= c0[:, None]) & (col[None, :] < c0[:, None] + NA_COLS)
    dc_idx = jnp.clip(col[None, :] - col[:, None] + NA_COLS - 1, 0, 2 * NA_COLS - 2)

    def row_block(r):
        r0 = jnp.clip(r - kr // 2, 0, rows - kr)
        qr = lax.dynamic_index_in_dim(q, r, axis=1, keepdims=False)
        kb = lax.dynamic_slice_in_dim(k, r0, kr, axis=1)
        vb = lax.dynamic_slice_in_dim(v, r0, kr, axis=1)
        s_lat = jnp.einsum('bqhd,brkhd->bhqrk', qr, kb, preferred_element_type=jnp.float32) * scale
        dr = r0 + jnp.arange(kr) - r + NA_ROWS - 1
        bias = jnp.take(jnp.take(rpb, dr, axis=1), dc_idx, axis=2)
        s_lat = s_lat + jnp.transpose(bias, (0, 2, 1, 3)).astype(jnp.float32)[None]
        s_lat = jnp.where(col_mask[None, None, :, None, :], s_lat, -jnp.inf)
        s_ctx = jnp.einsum('bqhd,blhd->bhql', qr, kc, preferred_element_type=jnp.float32) * scale
        s_all = jnp.concatenate([s_lat.reshape(bsz, NA_HEADS, GRID_W, kr * GRID_W), s_ctx], axis=-1)
        p = jax.nn.softmax(s_all, axis=-1).astype(v.dtype)
        p_lat = p[..., :kr * GRID_W].reshape(bsz, NA_HEADS, GRID_W, kr, GRID_W)
        p_ctx = p[..., kr * GRID_W:]
        return (jnp.einsum('bhqrk,brkhd->bqhd', p_lat, vb)
                + jnp.einsum('bhql,blhd->bqhd', p_ctx, vc))

    o = lax.map(row_block, jnp.arange(rows))
    o = jnp.transpose(o, (1, 0, 2, 3, 4)).reshape(bsz, n_lat, d)
    yx = o @ w_o + b_o
    if ctx_queries:
        qc = (hc @ w_qkv[:, :d] + b_qkv[:d]).reshape(bsz, n_ctx, NA_HEADS, NA_HEAD_DIM)
        sc = jnp.einsum('bqhd,bkhd->bhqk', qc, kc, preferred_element_type=jnp.float32) * scale
        oc = jnp.einsum('bhqk,bkhd->bqhd', jax.nn.softmax(sc, axis=-1).astype(vc.dtype), vc)
        yc = oc.reshape(bsz, n_ctx, d) @ w_o + b_o
    else:
        yc = None
    return yx, yc


def expert_swiglu(xb, w1, b1, w2, b2):
    hg = xb @ w1 + b1
    gate, up = hg[:, :D_EXPERT], hg[:, D_EXPERT:]
    gate = jnp.minimum(gate, SWIGLU_LIMIT)
    up = jnp.clip(up, -SWIGLU_LIMIT, SWIGLU_LIMIT)
    glu = gate * jax.nn.sigmoid(SWIGLU_ALPHA * gate)
    return (glu * (up + 1)) @ w2 + b2


def moe_ffn(h, w_router, b_router, w1, b1, w2, b2):
    n_tok = h.shape[0]
    logits = jnp.dot(h, w_router, preferred_element_type=jnp.float32) + b_router.astype(jnp.float32)
    top_logit, top_idx = lax.top_k(logits, TOP_K)
    gates = jax.nn.softmax(top_logit, axis=-1)
    n_assign = n_tok * TOP_K
    flat_e = top_idx.reshape(n_assign)
    flat_tok = jnp.arange(n_assign, dtype=jnp.int32) // TOP_K
    flat_gate = gates.reshape(n_assign)
    order = jnp.argsort(flat_e)
    sorted_e = flat_e[order]
    counts = jnp.bincount(flat_e, length=N_EXPERTS)
    starts = jnp.cumsum(counts) - counts
    padded = (counts + MOE_BLOCK - 1) // MOE_BLOCK * MOE_BLOCK
    pad_ends = jnp.cumsum(padded)
    pad_starts = pad_ends - padded
    dest = pad_starts[sorted_e] + jnp.arange(n_assign, dtype=jnp.int32) - starts[sorted_e]
    n_blocks = -(-n_assign // MOE_BLOCK) + N_EXPERTS
    n_slots = n_blocks * MOE_BLOCK
    slot_tok = jnp.zeros((n_slots,), jnp.int32).at[dest].set(flat_tok[order])
    slot_gate = jnp.zeros((n_slots,), jnp.float32).at[dest].set(flat_gate[order])
    block_expert = jnp.minimum(
        jnp.searchsorted(pad_ends, jnp.arange(n_blocks, dtype=jnp.int32) * MOE_BLOCK, side='right'),
        N_EXPERTS - 1)

    def block_step(acc, blk):
        tok, gate, e = blk
        y = expert_swiglu(h[tok], w1[e], b1[e], w2[e], b2[e])
        return acc.at[tok].add(y.astype(jnp.float32) * gate[:, None]), None

    acc, _ = lax.scan(block_step, jnp.zeros(h.shape, jnp.float32),
                      (slot_tok.reshape(n_blocks, MOE_BLOCK), slot_gate.reshape(n_blocks, MOE_BLOCK), block_expert))
    return acc.astype(h.dtype)


def setup_inputs(seed: int = 0) -> dict:
    key = jax.random.key(seed)
    ks = jax.random.split(key, 24)
    d = D_MODEL
    n_pool, n_na, n_fnet = n_layers_of_kind(0), n_layers_of_kind(1), n_layers_of_kind(2)

    def nrm(k, shape, s):
        return jax.random.normal(k, shape, jnp.float32) * s

    return {
        'x': nrm(ks[0], (BATCH, SEQ, d), 1.0),
        'c': nrm(ks[1], (BATCH, d), 1.0),
        'ctx': nrm(ks[2], (BATCH, CTX_LEN, d), 1.0),
        'c_ctx': nrm(ks[3], (d,), 1.0),
        'w_mod': nrm(ks[4], (DEPTH, d, N_MOD * d), 0.5 * d ** -0.5),
        'b_mod': nrm(ks[5], (DEPTH, N_MOD * d), 0.02),
        'norm_gain': 1.0 + nrm(ks[6], (DEPTH, 2, d), 0.05),
        'final_gain': 1.0 + nrm(ks[7], (d,), 0.05),
        'pool_w': nrm(ks[8], (n_pool, POOL_GROUPS, POOL_GROUP_DIM, POOL_GROUP_DIM), POOL_GROUP_DIM ** -0.5),
        'pool_scale': 1.0 + nrm(ks[9], (n_pool, d), 0.1),
        'na_w_qkv': nrm(ks[10], (n_na, d, 3 * d), d ** -0.5),
        'na_b_qkv': nrm(ks[11], (n_na, 3 * d), 0.02),
        'na_rpb': nrm(ks[12], (n_na, NA_HEADS, 2 * NA_ROWS - 1, 2 * NA_COLS - 1), 0.1),
        'na_w_o': nrm(ks[13], (n_na, d, d), d ** -0.5),
        'na_b_o': nrm(ks[14], (n_na, d), 0.02),
        'fnet_w': nrm(ks[15], (n_fnet, d, d), d ** -0.5),
        'fnet_b': nrm(ks[16], (n_fnet, d), 0.02),
        'router_w': nrm(ks[17], (DEPTH, d, N_EXPERTS), d ** -0.5),
        'router_b': nrm(ks[18], (DEPTH, N_EXPERTS), 0.01),
        'exp_w1': nrm(ks[19], (DEPTH, N_EXPERTS, d, 2 * D_EXPERT), d ** -0.5),
        'exp_b1': nrm(ks[20], (DEPTH, N_EXPERTS, 2 * D_EXPERT), 0.02),
        'exp_w2': nrm(ks[21], (DEPTH, N_EXPERTS, D_EXPERT, d), D_EXPERT ** -0.5),
        'exp_b2': nrm(ks[22], (DEPTH, N_EXPERTS, d), 0.02),
    }


def reference(x, c, ctx, c_ctx, w_mod, b_mod, norm_gain, final_gain, pool_w, pool_scale,
              na_w_qkv, na_b_qkv, na_rpb, na_w_o, na_b_o, fnet_w, fnet_b,
              router_w, router_b, exp_w1, exp_b1, exp_w2, exp_b2):
    bsz, n_lat, d = x.shape
    last_ctx_layer = ((DEPTH - 2) // N_MIXERS) * N_MIXERS + 1
    silu_c = jax.nn.silu(c)
    silu_cc = jax.nn.silu(c_ctx)
    for i in range(DEPTH):
        kind, j = i % N_MIXERS, i // N_MIXERS
        ctx_in = i <= last_ctx_layer
        ctx_out = i < last_ctx_layer
        mx = jnp.split((silu_c @ w_mod[i] + b_mod[i])[:, None, :], N_MOD, axis=-1)
        mc = jnp.split(silu_cc @ w_mod[i] + b_mod[i], N_MOD, axis=-1) if ctx_in else None

        hx = modulate(rms_norm(x, norm_gain[i, 0]), mx[0], mx[1])
        hc = modulate(rms_norm(ctx, norm_gain[i, 0]), mc[0], mc[1]) if ctx_in else None
        if kind == 0:
            yx = multiscale_pool(hx, pool_w[j], pool_scale[j])
            yc = multiscale_pool(hc, pool_w[j], pool_scale[j]) if ctx_out else None
        elif kind == 1:
            yx, yc = neighbourhood_attention(hx, hc, na_w_qkv[j], na_b_qkv[j], na_rpb[j],
                                             na_w_o[j], na_b_o[j], ctx_out)
        else:
            yx = fourier_mix(hx, fnet_w[j], fnet_b[j])
            yc = fourier_mix(hc, fnet_w[j], fnet_b[j]) if ctx_out else None
        x = x + mx[2] * yx
        if ctx_out:
            ctx = ctx + mc[2] * yc

        tokens = modulate(rms_norm(x, norm_gain[i, 1]), mx[3], mx[4]).reshape(bsz * n_lat, d)
        if ctx_out:
            hc2 = modulate(rms_norm(ctx, norm_gain[i, 1]), mc[3], mc[4])
            tokens = jnp.concatenate([tokens, hc2.reshape(-1, d)], axis=0)
        y = moe_ffn(tokens, router_w[i], router_b[i], exp_w1[i], exp_b1[i], exp_w2[i], exp_b2[i])
        x = x + mx[5] * y[:bsz * n_lat].reshape(bsz, n_lat, d)
        if ctx_out:
            ctx = ctx + mc[5] * y[bsz * n_lat:].reshape(ctx.shape)
    return rms_norm(x, final_gain)
```

```python
import functools

import jax
import jax.numpy as jnp
from jax import lax
from jax.experimental import pallas as pl
from jax.experimental.pallas import tpu as pltpu

F32 = jnp.float32
BF16 = jnp.bfloat16
I32 = jnp.int32

GRID_W = 64
N_MIXERS = 3
N_MOD = 6
RMS_EPS = 1e-6
POOL_WINDOWS = (2, 4, 8, 16)
POOL_HALO = 8
NA_HEAD_DIM = 32
NA_ROWS = 8
NA_COLS = 16
FNET_GROUPS = 4
TOP_K = 4
SWIGLU_LIMIT = 7.0
SWIGLU_ALPHA = 1.702
MOE_BLOCK = 256
MASK_VALUE = -1e30

LANES = 128
HEADS_PER_LANE_GROUP = LANES // NA_HEAD_DIM
MIB = 1024 * 1024


def _params(semantics, vmem_mib):
    return pltpu.CompilerParams(dimension_semantics=semantics, vmem_limit_bytes=vmem_mib * MIB)


def _norm_mod(x, gain, shift, scale):
    ms = jnp.mean(x * x, axis=-1, keepdims=True)
    y = x * lax.rsqrt(ms + RMS_EPS) * gain
    return y * (1.0 + scale) + shift


def _nt_dot(a, b):
    return lax.dot_general(a, b, (((1,), (1,)), ((), ())), preferred_element_type=F32)


def _mod_kernel(c_ref, w_ref, b_ref, o_ref):
    c = c_ref[...]
    s = (c * jax.nn.sigmoid(c)).astype(BF16)
    o_ref[0] = jnp.dot(s, w_ref[0].astype(BF16), preferred_element_type=F32) + b_ref[0]


def _modulation(c_all, w_mod, b_mod):
    depth, d, n = w_mod.shape
    nb = c_all.shape[0]
    tn = 1024
    return pl.pallas_call(
        _mod_kernel,
        grid=(depth, n // tn),
        in_specs=[pl.BlockSpec((nb, d), lambda i, j: (0, 0)),
                  pl.BlockSpec((1, d, tn), lambda i, j: (i, 0, j)),
                  pl.BlockSpec((1, 1, tn), lambda i, j: (i, 0, j))],
        out_specs=pl.BlockSpec((1, nb, tn), lambda i, j: (i, 0, j)),
        out_shape=jax.ShapeDtypeStruct((depth, nb, n), F32),
        compiler_params=_params(("arbitrary", "arbitrary"), 40),
        name="modulation",
    )(c_all, w_mod, b_mod.reshape(depth, 1, n))


class _Layout:
    def __init__(self, bsz, s_len, l_len, with_ctx):
        self.bsz, self.s_len, self.l_len, self.with_ctx = bsz, s_len, l_len, with_ctx
        self.n_lat = bsz * s_len
        self.n_tok = self.n_lat + (bsz * l_len if with_ctx else 0)

    def tile(self, want, seq_local=False):
        tm = min(want, self.s_len)
        if self.with_ctx:
            ctx_span = self.l_len if seq_local else self.bsz * self.l_len
            tm = min(tm, ctx_span)
            assert ctx_span % tm == 0
        assert self.s_len % tm == 0
        return tm

    def mod_row(self, t, tm):
        lat = (t * tm) // self.s_len
        if not self.with_ctx:
            return lat
        return jnp.where(t < self.n_lat // tm, lat, self.bsz)


def _mod_spec(mod3, layer, which, nb, layout, tm):
    base = (layer * N_MOD + which) * nb
    d = mod3.shape[-1]
    return pl.BlockSpec((1, 1, d), lambda t, *_: (base + layout.mod_row(t, tm), 0, 0))


def _pool_kernel(xc_ref, xp_ref, xn_ref, gain_ref, sh_ref, sc_ref, gt_ref, pw_ref, ps_ref, o_ref, ext_ref,
                 *, tm, s_len, l_len, n_lat_tiles, with_ctx):
    t = pl.program_id(0)
    if with_ctx:
        is_lat = t < n_lat_tiles
        tiles_per_seq = jnp.where(is_lat, s_len // tm, l_len // tm)
        tile_in_seq = jnp.where(is_lat, t % (s_len // tm), (t - n_lat_tiles) % (l_len // tm))
        seq_len = jnp.where(is_lat, s_len, l_len)
    else:
        tiles_per_seq = s_len // tm
        tile_in_seq = t % (s_len // tm)
        seq_len = s_len
    gain, shift, scale = gain_ref[...], sh_ref[0], sc_ref[0]
    hc = _norm_mod(xc_ref[...], gain, shift, scale)
    hp = _norm_mod(xp_ref[...], gain, shift, scale)
    hn = _norm_mod(xn_ref[...], gain, shift, scale)
    ext_ref[0:POOL_HALO] = jnp.where(tile_in_seq > 0, hp, 0.0)
    ext_ref[POOL_HALO:POOL_HALO + tm] = hc
    ext_ref[POOL_HALO + tm:2 * POOL_HALO + tm] = jnp.where(tile_in_seq < tiles_per_seq - 1, hn, 0.0)
    pos = tile_in_seq * tm + lax.broadcasted_iota(I32, (tm, 1), 0)
    dg = hc.shape[1] // len(POOL_WINDOWS)
    gate, pscale = gt_ref[0], ps_ref[...]
    for g, w in enumerate(POOL_WINDOWS):
        cs = slice(g * dg, (g + 1) * dg)
        start = POOL_HALO - w // 2
        acc = ext_ref[start:start + tm, cs]
        for j in range(1, w):
            acc = acc + ext_ref[start + j:start + j + tm, cs]
        cnt = jnp.minimum(pos + w // 2, seq_len) - jnp.maximum(pos - w // 2, 0)
        diff = (acc / cnt.astype(F32) - hc[:, cs]).astype(BF16)
        y = jnp.dot(diff, pw_ref[g], preferred_element_type=F32)
        o_ref[:, cs] = xc_ref[:, cs] + gate[:, cs] * (y * pscale[:, cs])


def _pool_mixer(xt, layout, mod3, nb, layer, gain, pool_w, pool_scale):
    n_tok, d = xt.shape
    tm = layout.tile(256, seq_local=True)
    hb = tm // POOL_HALO
    n_halo_blocks = n_tok // POOL_HALO
    kern = functools.partial(_pool_kernel, tm=tm, s_len=layout.s_len, l_len=layout.l_len,
                             n_lat_tiles=layout.n_lat // tm, with_ctx=layout.with_ctx)
    return pl.pallas_call(
        kern,
        grid=(layout.n_tok // tm,),
        in_specs=[pl.BlockSpec((tm, d), lambda t: (t, 0)),
                  pl.BlockSpec((POOL_HALO, d), lambda t: (jnp.maximum(t * hb - 1, 0), 0)),
                  pl.BlockSpec((POOL_HALO, d), lambda t: (jnp.minimum((t + 1) * hb, n_halo_blocks - 1), 0)),
                  pl.BlockSpec((1, d), lambda t: (0, 0)),
                  _mod_spec(mod3, layer, 0, nb, layout, tm),
                  _mod_spec(mod3, layer, 1, nb, layout, tm),
                  _mod_spec(mod3, layer, 2, nb, layout, tm),
                  pl.BlockSpec(pool_w.shape, lambda t: (0, 0, 0)),
                  pl.BlockSpec((1, d), lambda t: (0, 0))],
        out_specs=pl.BlockSpec((tm, d), lambda t: (t, 0)),
        out_shape=jax.ShapeDtypeStruct((layout.n_tok, d), F32),
        scratch_shapes=[pltpu.VMEM((tm + 2 * POOL_HALO, d), F32)],
        compiler_params=_params(("arbitrary",), 40),
        name="pool_mixer",
    )(xt, xt, xt, gain.reshape(1, d), mod3, mod3, mod3, pool_w.astype(BF16), pool_scale.reshape(1, d))


def _router_kernel(x_ref, gain_ref, sh_ref, sc_ref, wh_ref, wl_ref, rb_ref, h_ref, idx_ref, gate_ref):
    h = _norm_mod(x_ref[...], gain_ref[...], sh_ref[0], sc_ref[0])
    h_ref[...] = h
    h_hi = h.astype(BF16)
    h_lo = (h - h_hi.astype(F32)).astype(BF16)
    wh, wl = wh_ref[...], wl_ref[...]
    logits = _nt_dot(wh, h_hi) + _nt_dot(wh, h_lo) + _nt_dot(wl, h_hi) + rb_ref[...]
    n_exp = logits.shape[0]
    e_iota = lax.broadcasted_iota(I32, logits.shape, 0)
    work = logits
    vals, idxs = [], []
    for _ in range(TOP_K):
        m = jnp.max(work, axis=0, keepdims=True)
        idx = jnp.min(jnp.where(work == m, e_iota, n_exp), axis=0, keepdims=True)
        work = jnp.where(e_iota == idx, -jnp.inf, work)
        vals.append(m)
        idxs.append(idx)
    exps = [jnp.exp(v - vals[0]) for v in vals]
    denom = exps[0] + exps[1] + exps[2] + exps[3]
    pad_rows = idx_ref.shape[0] - TOP_K
    idx_ref[...] = jnp.concatenate(idxs + [jnp.zeros((pad_rows, logits.shape[1]), I32)], axis=0)
    gate_ref[...] = jnp.concatenate([e / denom for e in exps] + [jnp.zeros((pad_rows, logits.shape[1]), F32)],
                                    axis=0)


def _router(xt, layout, mod3, nb, layer, gain, router_w, router_b):
    n_tok, d = xt.shape
    n_exp = router_w.shape[1]
    tm = layout.tile(512)
    wt = router_w.T
    w_hi = wt.astype(BF16)
    w_lo = (wt - w_hi.astype(F32)).astype(BF16)
    return pl.pallas_call(
        _router_kernel,
        grid=(layout.n_tok // tm,),
        in_specs=[pl.BlockSpec((tm, d), lambda t: (t, 0)),
                  pl.BlockSpec((1, d), lambda t: (0, 0)),
                  _mod_spec(mod3, layer, 3, nb, layout, tm),
                  _mod_spec(mod3, layer, 4, nb, layout, tm),
                  pl.BlockSpec((n_exp, d), lambda t: (0, 0)),
                  pl.BlockSpec((n_exp, d), lambda t: (0, 0)),
                  pl.BlockSpec((n_exp, 1), lambda t: (0, 0))],
        out_specs=[pl.BlockSpec((tm, d), lambda t: (t, 0)),
                   pl.BlockSpec((8, tm), lambda t: (0, t)),
                   pl.BlockSpec((8, tm), lambda t: (0, t))],
        out_shape=[jax.ShapeDtypeStruct((layout.n_tok, d), F32),
                   jax.ShapeDtypeStruct((8, layout.n_tok), I32),
                   jax.ShapeDtypeStruct((8, layout.n_tok), F32)],
        compiler_params=_params(("arbitrary",), 40),
        name="moe_router",
    )(xt, gain.reshape(1, d), mod3, mod3, w_hi, w_lo, router_b.reshape(n_exp, 1))


def _rank_kernel(idx_ref, rank_ref, cnt_ref, run_ref, *, n_exp):
    t = pl.program_id(0)

    @pl.when(t == 0)
    def _():
        run_ref[...] = jnp.zeros_like(run_ref)

    tm = idx_ref.shape[1]
    e_iota = lax.broadcasted_iota(I32, (n_exp, tm), 0)
    sel = [e_iota == idx_ref[k:k + 1, :] for k in range(TOP_K)]
    member = sel[0] | sel[1] | sel[2] | sel[3]
    member_f = jnp.where(member, 1.0, 0.0)
    strictly_before = (lax.broadcasted_iota(I32, (tm, tm), 0) < lax.broadcasted_iota(I32, (tm, tm), 1))
    prefix = jnp.dot(member_f.astype(BF16), jnp.where(strictly_before, 1.0, 0.0).astype(BF16),
                     preferred_element_type=F32)
    base = run_ref[:, 0:1] + prefix
    ranks = [jnp.sum(jnp.where(s, base, 0.0), axis=0, keepdims=True).astype(I32) for s in sel]
    pad_rows = rank_ref.shape[0] - TOP_K
    rank_ref[...] = jnp.concatenate(ranks + [jnp.zeros((pad_rows, tm), I32)], axis=0)
    run_ref[...] = run_ref[...] + jnp.sum(member_f, axis=1, keepdims=True)
    cnt_ref[...] = run_ref[...].astype(I32)


def _ranks(idx_t, n_exp):
    n_tok = idx_t.shape[1]
    tm = 512 if n_tok % 512 == 0 else 256
    return pl.pallas_call(
        functools.partial(_rank_kernel, n_exp=n_exp),
        grid=(n_tok // tm,),
        in_specs=[pl.BlockSpec((8, tm), lambda t: (0, t))],
        out_specs=[pl.BlockSpec((8, tm), lambda t: (0, t)),
                   pl.BlockSpec((n_exp, LANES), lambda t: (0, 0))],
        out_shape=[jax.ShapeDtypeStruct((8, n_tok), I32),
                   jax.ShapeDtypeStruct((n_exp, LANES), I32)],
        scratch_shapes=[pltpu.VMEM((n_exp, LANES), F32)],
        compiler_params=_params(("arbitrary",), 32),
        name="moe_rank",
    )(idx_t)


def _dest_kernel(idx_ref, rank_ref, ps_ref, dest_ref, *, n_exp):
    tm = idx_ref.shape[1]
    e_iota = lax.broadcasted_iota(I32, (n_exp, tm), 0)
    ps = ps_ref[:, 0:1]
    rows = []
    for k in range(TOP_K):
        start = jnp.sum(jnp.where(e_iota == idx_ref[k:k + 1, :], ps, 0.0), axis=0, keepdims=True)
        rows.append(start.astype(I32) + rank_ref[k:k + 1, :])
    pad_rows = dest_ref.shape[0] - TOP_K
    dest_ref[...] = jnp.concatenate(rows + [jnp.zeros((pad_rows, tm), I32)], axis=0)


def _dests(idx_t, rank_t, pad_starts):
    n_tok = idx_t.shape[1]
    n_exp = pad_starts.shape[0]
    tm = 512 if n_tok % 512 == 0 else 256
    ps = jnp.broadcast_to(pad_starts.astype(F32)[:, None], (n_exp, LANES))
    return pl.pallas_call(
        functools.partial(_dest_kernel, n_exp=n_exp),
        grid=(n_tok // tm,),
        in_specs=[pl.BlockSpec((8, tm), lambda t: (0, t)),
                  pl.BlockSpec((8, tm), lambda t: (0, t)),
                  pl.BlockSpec((n_exp, LANES), lambda t: (0, 0))],
        out_specs=pl.BlockSpec((8, tm), lambda t: (0, t)),
        out_shape=jax.ShapeDtypeStruct((8, n_tok), I32),
        compiler_params=_params(("arbitrary",), 32),
        name="moe_dest",
    )(idx_t, rank_t, ps)


def _tile_major(dest_t, tm):
    n_tok = dest_t.shape[1]
    return dest_t[:TOP_K].reshape(TOP_K, n_tok // tm, tm).transpose(1, 0, 2).reshape(-1)


def _dispatch_kernel(cnt_ref, ps_ref, h_ref, dest_ref, xs_ref, zero_ref, sem, zsem, *, n_exp, block):
    t = pl.program_id(0)
    tm = h_ref.shape[0]

    def row_copy(r, d):
        return pltpu.make_async_copy(h_ref.at[pl.ds(r, 1)], xs_ref.at[pl.ds(d, 1)], sem)

    def issue(r, carry):
        for k in range(TOP_K):
            row_copy(r, dest_ref[k * tm + r]).start()
        return carry

    lax.fori_loop(0, tm, issue, 0)

    @pl.when(t == 0)
    def _():
        zero_ref[...] = jnp.zeros_like(zero_ref)

        def zero_copy(d):
            return pltpu.make_async_copy(zero_ref.at[pl.ds(0, 1)], xs_ref.at[pl.ds(d, 1)], zsem)

        for e in range(n_exp):
            cnt = cnt_ref[e]
            first = ps_ref[e] + cnt
            n_pad = (-cnt) % block

            def zissue(i, carry, first=first):
                zero_copy(first + i).start()
                return carry

            def zwait(i, carry):
                zero_copy(0).wait()
                return carry

            lax.fori_loop(0, n_pad, zissue, 0)
            lax.fori_loop(0, n_pad, zwait, 0)

        def tail_copy(j):
            return pltpu.make_async_copy(zero_ref, xs_ref.at[pl.ds(j * block, block)], zsem)

        def tail_issue(j, carry):
            tail_copy(j).start()
            return carry

        def tail_wait(j, carry):
            tail_copy(j).wait()
            return carry

        first_tail = ps_ref[n_exp] // block
        lax.fori_loop(first_tail, xs_ref.shape[0] // block, tail_issue, 0)
        lax.fori_loop(first_tail, xs_ref.shape[0] // block, tail_wait, 0)

    def wait(i, carry):
        row_copy(0, 0).wait()
        return carry

    lax.fori_loop(0, TOP_K * tm, wait, 0)


def _dispatch(h, dest_flat, counts, pad_starts, n_slots, tm):
    n_tok, d = h.shape
    n_exp = counts.shape[0]
    grid_spec = pltpu.PrefetchScalarGridSpec(
        num_scalar_prefetch=2,
        grid=(n_tok // tm,),
        in_specs=[pl.BlockSpec((tm, d), lambda t, *_: (t, 0)),
                  pl.BlockSpec((TOP_K * tm,), lambda t, *_: (t,), memory_space=pltpu.SMEM)],
        out_specs=pl.BlockSpec(memory_space=pl.ANY),
        scratch_shapes=[pltpu.VMEM((MOE_BLOCK, d), F32), pltpu.SemaphoreType.DMA, pltpu.SemaphoreType.DMA],
    )
    return pl.pallas_call(
        functools.partial(_dispatch_kernel, n_exp=n_exp, block=MOE_BLOCK),
        grid_spec=grid_spec,
        out_shape=jax.ShapeDtypeStruct((n_slots, d), F32),
        compiler_params=_params(("arbitrary",), 32),
        name="moe_dispatch",
    )(counts, pad_starts, h, dest_flat)


def _expert_kernel(be_ref, nu_ref, x_ref, w1_ref, b1_ref, w2_ref, b2_ref, y_ref):
    j = pl.program_id(0)

    @pl.when(j < nu_ref[0])
    def _():
        f = w2_ref.shape[1]
        hg = jnp.dot(x_ref[...].astype(BF16), w1_ref[0], preferred_element_type=F32) + b1_ref[0]
        gate = jnp.minimum(hg[:, :f], SWIGLU_LIMIT)
        up = jnp.clip(hg[:, f:], -SWIGLU_LIMIT, SWIGLU_LIMIT)
        glu = gate * jax.nn.sigmoid(SWIGLU_ALPHA * gate)
        act = (glu * (up + 1.0)).astype(BF16)
        y_ref[...] = jnp.dot(act, w2_ref[0], preferred_element_type=F32) + b2_ref[0]

    @pl.when(j >= nu_ref[0])
    def _():
        y_ref[...] = jnp.zeros_like(y_ref)


def _experts(xs, block_expert, n_used, w1, b1, w2, b2):
    n_slots, d = xs.shape
    n_exp, _, f2 = w1.shape
    f = w2.shape[1]
    bm = MOE_BLOCK

    def row_block(j, be, nu):
        return (jnp.minimum(j, nu[0] - 1), 0)

    grid_spec = pltpu.PrefetchScalarGridSpec(
        num_scalar_prefetch=2,
        grid=(n_slots // bm,),
        in_specs=[pl.BlockSpec((bm, d), row_block),
                  pl.BlockSpec((1, d, f2), lambda j, be, nu: (be[j], 0, 0)),
                  pl.BlockSpec((1, 1, f2), lambda j, be, nu: (be[j], 0, 0)),
                  pl.BlockSpec((1, f, d), lambda j, be, nu: (be[j], 0, 0)),
                  pl.BlockSpec((1, 1, d), lambda j, be, nu: (be[j], 0, 0))],
        out_specs=pl.BlockSpec((bm, d), lambda j, be, nu: (j, 0)),
    )
    return pl.pallas_call(
        _expert_kernel,
        grid_spec=grid_spec,
        out_shape=jax.ShapeDtypeStruct((n_slots, d), F32),
        compiler_params=_params(("arbitrary",), 48),
        name="moe_experts",
    )(block_expert, n_used, xs, w1, b1.reshape(n_exp, 1, f2), w2, b2.reshape(n_exp, 1, d))


def _combine_kernel(x_ref, g_ref, dest_ref, gt_ref, fg_ref, ys_ref, o_ref, buf_ref, sem, *, final_norm):
    tm = x_ref.shape[0]

    def row_copy(k, r, d):
        return pltpu.make_async_copy(ys_ref.at[pl.ds(d, 1)], buf_ref.at[k, pl.ds(r, 1)], sem)

    def issue(r, carry):
        for k in range(TOP_K):
            row_copy(k, r, dest_ref[k * tm + r]).start()
        return carry

    lax.fori_loop(0, tm, issue, 0)

    g8 = g_ref[...]
    g_cols = jnp.concatenate([g8, jnp.zeros((tm - 8, tm), F32)], axis=0).T

    def wait(i, carry):
        row_copy(0, 0, 0).wait()
        return carry

    lax.fori_loop(0, TOP_K * tm, wait, 0)

    acc = g_cols[:, 0:1] * buf_ref[0]
    for k in range(1, TOP_K):
        acc = acc + g_cols[:, k:k + 1] * buf_ref[k]
    out = x_ref[...] + gt_ref[0] * acc
    if final_norm:
        ms = jnp.mean(out * out, axis=-1, keepdims=True)
        out = out * lax.rsqrt(ms + RMS_EPS) * fg_ref[...]
    o_ref[...] = out


def _combine(xt, out_layout, gates_t, dest_flat, ys, mod3, nb, layer, final_gain, tm):
    d = xt.shape[1]
    final_norm = final_gain is not None
    fg = (final_gain if final_norm else jnp.ones((d,), F32)).reshape(1, d)
    return pl.pallas_call(
        functools.partial(_combine_kernel, final_norm=final_norm),
        grid=(out_layout.n_tok // tm,),
        in_specs=[pl.BlockSpec((tm, d), lambda t: (t, 0)),
                  pl.BlockSpec((8, tm), lambda t: (0, t)),
                  pl.BlockSpec((TOP_K * tm,), lambda t: (t,), memory_space=pltpu.SMEM),
                  _mod_spec(mod3, layer, 5, nb, out_layout, tm),
                  pl.BlockSpec((1, d), lambda t: (0, 0)),
                  pl.BlockSpec(memory_space=pl.ANY)],
        out_specs=pl.BlockSpec((tm, d), lambda t: (t, 0)),
        out_shape=jax.ShapeDtypeStruct((out_layout.n_tok, d), F32),
        scratch_shapes=[pltpu.VMEM((TOP_K, tm, d), F32), pltpu.SemaphoreType.DMA],
        compiler_params=_params(("arbitrary",), 32),
        name="moe_combine",
    )(xt, gates_t, dest_flat, mod3, fg, ys)


def _moe(xt, layout, out_layout, mod3, nb, layer, gain, router_w, router_b, w1, b1, w2, b2, final_gain):
    n_exp = router_w.shape[1]
    h, idx_t, gates_t = _router(xt, layout, mod3, nb, layer, gain, router_w, router_b)
    rank_t, cnt = _ranks(idx_t, n_exp)
    counts = cnt[:, 0]
    padded = (counts + MOE_BLOCK - 1) // MOE_BLOCK * MOE_BLOCK
    pad_ends = jnp.cumsum(padded)
    pad_starts = pad_ends - padded
    n_blocks = -(-layout.n_tok * TOP_K // MOE_BLOCK) + n_exp
    n_slots = n_blocks * MOE_BLOCK
    block_expert = jnp.minimum(
        jnp.searchsorted(pad_ends, jnp.arange(n_blocks, dtype=I32) * MOE_BLOCK, side="right"),
        n_exp - 1).astype(I32)
    n_used = (pad_ends[-1:] // MOE_BLOCK).astype(I32)
    dest_t = _dests(idx_t, rank_t, pad_starts)
    tm_d = layout.tile(256)
    tm_c = tm_d
    slot_bounds = jnp.concatenate([pad_starts, pad_ends[-1:]]).astype(I32)
    xs = _dispatch(h, _tile_major(dest_t, tm_d), counts, slot_bounds, n_slots, tm_d)
    ys = _experts(xs, block_expert, n_used, w1.astype(BF16), b1, w2.astype(BF16), b2)
    return _combine(xt, out_layout, gates_t, _tile_major(dest_t, tm_c), ys, mod3, nb, layer, final_gain, tm_c)


def _nm_matmul_kernel(x_ref, gain_ref, sh_ref, sc_ref, w_ref, b_ref, o_ref, h_ref):
    @pl.when(pl.program_id(1) == 0)
    def _():
        h_ref[...] = _norm_mod(x_ref[...], gain_ref[...], sh_ref[0], sc_ref[0]).astype(BF16)

    o_ref[...] = (jnp.dot(h_ref[...], w_ref[...], preferred_element_type=F32) + b_ref[...]).astype(o_ref.dtype)


def _nm_matmul(xt, layout, mod3, nb, layer, gain, w, b, tm_want, tn):
    d = xt.shape[1]
    n = w.shape[1]
    tm = layout.tile(tm_want)
    return pl.pallas_call(
        _nm_matmul_kernel,
        grid=(layout.n_tok // tm, n // tn),
        in_specs=[pl.BlockSpec((tm, d), lambda m, j: (m, 0)),
                  pl.BlockSpec((1, d), lambda m, j: (0, 0)),
                  _mod_spec(mod3, layer, 0, nb, layout, tm),
                  _mod_spec(mod3, layer, 1, nb, layout, tm),
                  pl.BlockSpec((d, tn), lambda m, j: (0, j)),
                  pl.BlockSpec((1, tn), lambda m, j: (0, j))],
        out_specs=pl.BlockSpec((tm, tn), lambda m, j: (m, j)),
        out_shape=jax.ShapeDtypeStruct((layout.n_tok, n), BF16),
        scratch_shapes=[pltpu.VMEM((tm, d), BF16)],
        compiler_params=_params(("arbitrary", "arbitrary"), 48),
        name="norm_mod_matmul",
    )(xt, gain.reshape(1, d), mod3, mod3, w.astype(BF16), b.reshape(1, n))


def _proj_residual_kernel(a_ref, w_ref, b_ref, x_ref, gt_ref, o_ref):
    y = jnp.dot(a_ref[...], w_ref[...], preferred_element_type=F32) + b_ref[...]
    o_ref[...] = x_ref[...] + gt_ref[0] * y


def _proj_residual(a, w, b, xt, out_layout, mod3, nb, layer):
    k, n = w.shape
    tm = out_layout.tile(512)
    return pl.pallas_call(
        _proj_residual_kernel,
        grid=(out_layout.n_tok // tm,),
        in_specs=[pl.BlockSpec((tm, k), lambda m: (m, 0)),
                  pl.BlockSpec((k, n), lambda m: (0, 0)),
                  pl.BlockSpec((1, n), lambda m: (0, 0)),
                  pl.BlockSpec((tm, n), lambda m: (m, 0)),
                  _mod_spec(mod3, layer, 2, nb, out_layout, tm)],
        out_specs=pl.BlockSpec((tm, n), lambda m: (m, 0)),
        out_shape=jax.ShapeDtypeStruct((out_layout.n_tok, n), F32),
        compiler_params=_params(("arbitrary",), 48),
        name="proj_residual",
    )(a, w.astype(BF16), b.reshape(1, n), xt, mod3)


def _na_bias_slabs(rpb, rows):
    n_heads = rpb.shape[0]
    kr = min(NA_ROWS, rows)
    col = jnp.arange(GRID_W)
    c0 = jnp.clip(col - NA_COLS // 2, 0, GRID_W - NA_COLS)
    col_mask = (col[None, :] >= c0[:, None]) & (col[None, :] < c0[:, None] + NA_COLS)
    dc_idx = jnp.clip(col[None, :] - col[:, None] + NA_COLS - 1, 0, 2 * NA_COLS - 2)
    n_place = NA_ROWS
    dr = jnp.arange(n_place)[:, None] + jnp.arange(kr)[None, :]
    bias = rpb[:, dr][:, :, :, dc_idx]
    bias = jnp.where(col_mask[None, None, None], bias, MASK_VALUE)
    bias = jnp.transpose(bias, (1, 0, 3, 2, 4))
    return bias.reshape(n_place, n_heads // HEADS_PER_LANE_GROUP, HEADS_PER_LANE_GROUP * GRID_W, kr * GRID_W)


def _na_kernel(q_ref, k_ref, v_ref, kc_ref, vc_ref, bias_ref, o_ref, *, rows, kr, n_groups):
    r = pl.program_id(2)
    r0 = jnp.clip(r - kr // 2, 0, rows - kr)
    win = pl.ds(pl.multiple_of(r0 * GRID_W, GRID_W), kr * GRID_W)
    scale = NA_HEAD_DIM ** -0.5
    nq = HEADS_PER_LANE_GROUP * GRID_W
    head_of_row = lax.broadcasted_iota(I32, (nq, LANES), 0) // GRID_W
    head_of_lane = lax.broadcasted_iota(I32, (nq, LANES), 1) // NA_HEAD_DIM
    own = head_of_row == head_of_lane
    for g in range(n_groups):
        gs = slice(g * LANES, (g + 1) * LANES)
        qg = q_ref[:, gs]
        qbd = jnp.where(own, jnp.concatenate([qg] * HEADS_PER_LANE_GROUP, axis=0), jnp.zeros((), BF16))
        s_lat = _nt_dot(qbd, k_ref[win, gs]) * scale + bias_ref[0, g]
        s_ctx = _nt_dot(qbd, kc_ref[:, gs]) * scale
        m = jnp.maximum(jnp.max(s_lat, axis=-1, keepdims=True), jnp.max(s_ctx, axis=-1, keepdims=True))
        p_lat = jnp.exp(s_lat - m)
        p_ctx = jnp.exp(s_ctx - m)
        denom = jnp.sum(p_lat, axis=-1, keepdims=True) + jnp.sum(p_ctx, axis=-1, keepdims=True)
        o = (jnp.dot(p_lat.astype(BF16), v_ref[win, gs], preferred_element_type=F32)
             + jnp.dot(p_ctx.astype(BF16), vc_ref[:, gs], preferred_element_type=F32)) / denom
        o = jnp.where(own, o, 0.0)
        og = o[0:GRID_W]
        for h in range(1, HEADS_PER_LANE_GROUP):
            og = og + o[h * GRID_W:(h + 1) * GRID_W]
        o_ref[:, gs] = og.astype(o_ref.dtype)


def _neighbourhood_attention(qkv, bias_slabs, bsz, s_len, l_len, d):
    rows = s_len // GRID_W
    kr = min(NA_ROWS, rows)
    n_split = 2
    dh = d // n_split
    n_groups = dh // LANES
    ctx_block0 = bsz * s_len // l_len

    def place(r):
        r0 = jnp.clip(r - kr // 2, 0, rows - kr)
        return r0 - r + NA_ROWS - 1

    return pl.pallas_call(
        functools.partial(_na_kernel, rows=rows, kr=kr, n_groups=n_groups),
        grid=(bsz, n_split, rows),
        in_specs=[pl.BlockSpec((GRID_W, dh), lambda b, h, r: (b * rows + r, h)),
                  pl.BlockSpec((s_len, dh), lambda b, h, r: (b, n_split + h)),
                  pl.BlockSpec((s_len, dh), lambda b, h, r: (b, 2 * n_split + h)),
                  pl.BlockSpec((l_len, dh), lambda b, h, r: (ctx_block0 + b, n_split + h)),
                  pl.BlockSpec((l_len, dh), lambda b, h, r: (ctx_block0 + b, 2 * n_split + h)),
                  pl.BlockSpec((1, n_groups, HEADS_PER_LANE_GROUP * GRID_W, kr * GRID_W),
                               lambda b, h, r: (place(r), h, 0, 0))],
        out_specs=pl.BlockSpec((GRID_W, dh), lambda b, h, r: (b * rows + r, h)),
        out_shape=jax.ShapeDtypeStruct((bsz * s_len, d), BF16),
        compiler_params=_params(("arbitrary", "arbitrary", "arbitrary"), 48),
        name="neighbourhood_attention",
    )(qkv, qkv, qkv, qkv, qkv, bias_slabs)


def _dft_cos_sin(n):
    j = jnp.arange(n, dtype=I32)
    ang = ((j[:, None] * j[None, :]) % n).astype(F32) * (2.0 * jnp.pi / n)
    norm = n ** -0.5
    return jnp.cos(ang) * norm, jnp.sin(ang) * norm


def _fnet_stage1_kernel(x_ref, gain_ref, sh_ref, sc_ref, cs_ref, o_ref):
    h = _norm_mod(x_ref[...], gain_ref[...], sh_ref[0], sc_ref[0]).astype(BF16)
    dg = cs_ref.shape[0]
    for g in range(FNET_GROUPS):
        cols = slice(g * dg, (g + 1) * dg)
        a = jnp.dot(h[:, cols], cs_ref[...], preferred_element_type=F32)
        o_ref[0, 0, :, cols] = a[:, :dg].astype(o_ref.dtype)
        o_ref[0, 1, :, cols] = a[:, dg:].astype(o_ref.dtype)


def _fnet_stage1(xt, layout, mod3, nb, layer, gain):
    d = xt.shape[1]
    dg = d // FNET_GROUPS
    tm = layout.tile(512)
    tiles_per_seq = layout.s_len // tm
    cc, sc = _dft_cos_sin(dg)
    cs = jnp.concatenate([cc, sc], axis=1).astype(BF16)
    return pl.pallas_call(
        _fnet_stage1_kernel,
        grid=(layout.n_lat // tm,),
        in_specs=[pl.BlockSpec((tm, d), lambda t: (t, 0)),
                  pl.BlockSpec((1, d), lambda t: (0, 0)),
                  _mod_spec(mod3, layer, 0, nb, layout, tm),
                  _mod_spec(mod3, layer, 1, nb, layout, tm),
                  pl.BlockSpec((dg, 2 * dg), lambda t: (0, 0))],
        out_specs=pl.BlockSpec((1, 2, tm, d), lambda t: (t // tiles_per_seq, 0, t % tiles_per_seq, 0)),
        out_shape=jax.ShapeDtypeStruct((layout.bsz, 2, layout.s_len, d), BF16),
        compiler_params=_params(("arbitrary",), 40),
        name="fnet_channel_dft",
    )(xt, gain.reshape(1, d), mod3, mod3, cs)


def _fnet_stage2_kernel(l_ref, a_ref, o_ref):
    o_ref[...] = jnp.dot(l_ref[...], a_ref[0], preferred_element_type=F32).astype(o_ref.dtype)


def _fnet_stage2(a, s_len):
    bsz, _, _, d = a.shape
    cn, sn = _dft_cos_sin(s_len)
    lhs = jnp.concatenate([cn, -sn], axis=1).astype(BF16)
    tm = min(512, s_len)
    tn = min(1024, d)
    m_tiles = s_len // tm
    return pl.pallas_call(
        _fnet_stage2_kernel,
        grid=(bsz, d // tn, m_tiles),
        in_specs=[pl.BlockSpec((tm, 2 * s_len), lambda b, j, m: (m, 0)),
                  pl.BlockSpec((1, 2 * s_len, tn), lambda b, j, m: (b, 0, j))],
        out_specs=pl.BlockSpec((tm, tn), lambda b, j, m: (b * m_tiles + m, j)),
        out_shape=jax.ShapeDtypeStruct((bsz * s_len, d), BF16),
        compiler_params=_params(("arbitrary", "arbitrary", "arbitrary"), 48),
        name="fnet_position_dft",
    )(lhs, a.reshape(bsz, 2 * s_len, d))


def kernel(x, c, ctx, c_ctx, w_mod, b_mod, norm_gain, final_gain, pool_w, pool_scale, na_w_qkv, na_b_qkv, na_rpb, na_w_o, na_b_o, fnet_w, fnet_b, router_w, router_b, exp_w1, exp_b1, exp_w2, exp_b2):
    bsz, s_len, d = x.shape
    l_len = ctx.shape[1]
    depth = w_mod.shape[0]
    last_ctx_layer = ((depth - 2) // N_MIXERS) * N_MIXERS + 1

    nb = -(-(bsz + 1) // 8) * 8
    c_all = jnp.concatenate([c, c_ctx[None], jnp.zeros((nb - bsz - 1, d), F32)], axis=0)
    mod = _modulation(c_all, w_mod, b_mod)
    mod3 = mod.reshape(depth, nb, N_MOD, d).transpose(0, 2, 1, 3).reshape(depth * N_MOD * nb, 1, d)

    lat_only = _Layout(bsz, s_len, l_len, False)
    with_ctx = _Layout(bsz, s_len, l_len, True)
    xt = jnp.concatenate([x.reshape(bsz * s_len, d), ctx.reshape(bsz * l_len, d)], axis=0)
    for i in range(depth):
        kind, j = i % N_MIXERS, i // N_MIXERS
        ctx_in = i <= last_ctx_layer
        ctx_out = i < last_ctx_layer
        layout_in = with_ctx if ctx_in else lat_only
        layout_out = with_ctx if ctx_out else lat_only
        if kind == 0:
            xt = _pool_mixer(xt, layout_out, mod3, nb, i, norm_gain[i, 0], pool_w[j], pool_scale[j])
        elif kind == 1:
            if ctx_in:
                qkv = _nm_matmul(xt, layout_in, mod3, nb, i, norm_gain[i, 0], na_w_qkv[j], na_b_qkv[j], 1024,
                                 min(1024, d))
                slabs = _na_bias_slabs(na_rpb[j], s_len // GRID_W)
                att = _neighbourhood_attention(qkv, slabs, bsz, s_len, l_len, d)
                xt = _proj_residual(att, na_w_o[j], na_b_o[j], xt, layout_out, mod3, nb, i)
            else:
                raise NotImplementedError("neighbourhood attention without a live context stream")
        else:
            a = _fnet_stage1(xt, layout_out, mod3, nb, i, norm_gain[i, 0])
            f = _fnet_stage2(a, s_len)
            xt = _proj_residual(f, fnet_w[j], fnet_b[j], xt, layout_out, mod3, nb, i)
        xt = _moe(xt, layout_out, layout_out, mod3, nb, i, norm_gain[i, 1], router_w[i], router_b[i],
                  exp_w1[i], exp_b1[i], exp_w2[i], exp_b2[i], final_gain if i == depth - 1 else None)
    return xt[:bsz * s_len].reshape(bsz, s_len, d)
```

```python
import functools

import jax
import jax.numpy as jnp
from jax import lax
from jax.experimental import pallas as pl
from jax.experimental.pallas import tpu as pltpu

F32 = jnp.float32
BF16 = jnp.bfloat16
I32 = jnp.int32

GRID_W = 64
N_MIXERS = 3
N_MOD = 6
RMS_EPS = 1e-6
POOL_WINDOWS = (2, 4, 8, 16)
POOL_HALO = 8
NA_HEAD_DIM = 32
NA_ROWS = 8
NA_COLS = 16
FNET_GROUPS = 4
TOP_K = 4
SWIGLU_LIMIT = 7.0
SWIGLU_ALPHA = 1.702
MOE_BLOCK = 256
MASK_VALUE = -1e30
ROW_DMA_UNROLL = 4

LANES = 128
HEADS_PER_LANE_GROUP = LANES // NA_HEAD_DIM
MIB = 1024 * 1024


def _params(semantics, vmem_mib):
    return pltpu.CompilerParams(dimension_semantics=semantics, vmem_limit_bytes=vmem_mib * MIB)


def _norm_mod(x, gain, shift, scale):
    ms = jnp.mean(x * x, axis=-1, keepdims=True)
    y = x * lax.rsqrt(ms + RMS_EPS) * gain
    return y * (1.0 + scale) + shift


def _nt_dot(a, b):
    return lax.dot_general(a, b, (((1,), (1,)), ((), ())), preferred_element_type=F32)


def _mod_kernel(c_ref, w_ref, b_ref, o_ref):
    c = c_ref[...]
    s = (c * jax.nn.sigmoid(c)).astype(BF16)
    o_ref[0] = jnp.dot(s, w_ref[0].astype(BF16), preferred_element_type=F32) + b_ref[0]


def _modulation(c_all, w_mod, b_mod):
    depth, d, n = w_mod.shape
    nb = c_all.shape[0]
    tn = 1024
    return pl.pallas_call(
        _mod_kernel,
        grid=(depth, n // tn),
        in_specs=[pl.BlockSpec((nb, d), lambda i, j: (0, 0)),
                  pl.BlockSpec((1, d, tn), lambda i, j: (i, 0, j)),
                  pl.BlockSpec((1, 1, tn), lambda i, j: (i, 0, j))],
        out_specs=pl.BlockSpec((1, nb, tn), lambda i, j: (i, 0, j)),
        out_shape=jax.ShapeDtypeStruct((depth, nb, n), F32),
        compiler_params=_params(("arbitrary", "arbitrary"), 40),
        name="modulation",
    )(c_all, w_mod, b_mod.reshape(depth, 1, n))


class _Layout:
    def __init__(self, bsz, s_len, l_len, with_ctx):
        self.bsz, self.s_len, self.l_len, self.with_ctx = bsz, s_len, l_len, with_ctx
        self.n_lat = bsz * s_len
        self.n_tok = self.n_lat + (bsz * l_len if with_ctx else 0)

    def tile(self, want, seq_local=False):
        tm = min(want, self.s_len)
        if self.with_ctx:
            ctx_span = self.l_len if seq_local else self.bsz * self.l_len
            tm = min(tm, ctx_span)
            assert ctx_span % tm == 0
        assert self.s_len % tm == 0
        return tm

    def mod_row(self, t, tm):
        lat = (t * tm) // self.s_len
        if not self.with_ctx:
            return lat
        return jnp.where(t < self.n_lat // tm, lat, self.bsz)


def _mod_spec(mod3, layer, which, nb, layout, tm):
    base = (layer * N_MOD + which) * nb
    d = mod3.shape[-1]
    return pl.BlockSpec((1, 1, d), lambda t, *_: (base + layout.mod_row(t, tm), 0, 0))


def _pool_kernel(xc_ref, xp_ref, xn_ref, gain_ref, sh_ref, sc_ref, gt_ref, pw_ref, ps_ref, o_ref, ext_ref,
                 *, tm, s_len, l_len, n_lat_tiles, with_ctx):
    t = pl.program_id(0)
    if with_ctx:
        is_lat = t < n_lat_tiles
        tiles_per_seq = jnp.where(is_lat, s_len // tm, l_len // tm)
        tile_in_seq = jnp.where(is_lat, t % (s_len // tm), (t - n_lat_tiles) % (l_len // tm))
        seq_len = jnp.where(is_lat, s_len, l_len)
    else:
        tiles_per_seq = s_len // tm
        tile_in_seq = t % (s_len // tm)
        seq_len = s_len
    gain, shift, scale = gain_ref[...], sh_ref[0], sc_ref[0]
    hc = _norm_mod(xc_ref[...], gain, shift, scale)
    hp = _norm_mod(xp_ref[...], gain, shift, scale)
    hn = _norm_mod(xn_ref[...], gain, shift, scale)
    ext_ref[0:POOL_HALO] = jnp.where(tile_in_seq > 0, hp, 0.0)
    ext_ref[POOL_HALO:POOL_HALO + tm] = hc
    ext_ref[POOL_HALO + tm:2 * POOL_HALO + tm] = jnp.where(tile_in_seq < tiles_per_seq - 1, hn, 0.0)
    pos = tile_in_seq * tm + lax.broadcasted_iota(I32, (tm, 1), 0)
    dg = hc.shape[1] // len(POOL_WINDOWS)
    gate, pscale = gt_ref[0], ps_ref[...]
    for g, w in enumerate(POOL_WINDOWS):
        cs = slice(g * dg, (g + 1) * dg)
        start = POOL_HALO - w // 2
        acc = ext_ref[start:start + tm, cs]
        for j in range(1, w):
            acc = acc + ext_ref[start + j:start + j + tm, cs]
        cnt = jnp.minimum(pos + w // 2, seq_len) - jnp.maximum(pos - w // 2, 0)
        diff = (acc / cnt.astype(F32) - hc[:, cs]).astype(BF16)
        y = jnp.dot(diff, pw_ref[g], preferred_element_type=F32)
        o_ref[:, cs] = xc_ref[:, cs] + gate[:, cs] * (y * pscale[:, cs])


def _pool_mixer(xt, layout, mod3, nb, layer, gain, pool_w, pool_scale):
    n_tok, d = xt.shape
    tm = layout.tile(256, seq_local=True)
    hb = tm // POOL_HALO
    n_halo_blocks = n_tok // POOL_HALO
    kern = functools.partial(_pool_kernel, tm=tm, s_len=layout.s_len, l_len=layout.l_len,
                             n_lat_tiles=layout.n_lat // tm, with_ctx=layout.with_ctx)
    return pl.pallas_call(
        kern,
        grid=(layout.n_tok // tm,),
        in_specs=[pl.BlockSpec((tm, d), lambda t: (t, 0)),
                  pl.BlockSpec((POOL_HALO, d), lambda t: (jnp.maximum(t * hb - 1, 0), 0)),
                  pl.BlockSpec((POOL_HALO, d), lambda t: (jnp.minimum((t + 1) * hb, n_halo_blocks - 1), 0)),
                  pl.BlockSpec((1, d), lambda t: (0, 0)),
                  _mod_spec(mod3, layer, 0, nb, layout, tm),
                  _mod_spec(mod3, layer, 1, nb, layout, tm),
                  _mod_spec(mod3, layer, 2, nb, layout, tm),
                  pl.BlockSpec(pool_w.shape, lambda t: (0, 0, 0)),
                  pl.BlockSpec((1, d), lambda t: (0, 0))],
        out_specs=pl.BlockSpec((tm, d), lambda t: (t, 0)),
        out_shape=jax.ShapeDtypeStruct((layout.n_tok, d), F32),
        scratch_shapes=[pltpu.VMEM((tm + 2 * POOL_HALO, d), F32)],
        compiler_params=_params(("arbitrary",), 40),
        name="pool_mixer",
    )(xt, xt, xt, gain.reshape(1, d), mod3, mod3, mod3, pool_w.astype(BF16), pool_scale.reshape(1, d))


def _router_kernel(x_ref, gain_ref, sh_ref, sc_ref, wh_ref, wl_ref, rb_ref, h_ref, idx_ref, gate_ref):
    h = _norm_mod(x_ref[...], gain_ref[...], sh_ref[0], sc_ref[0])
    h_ref[...] = h
    h_hi = h.astype(BF16)
    h_lo = (h - h_hi.astype(F32)).astype(BF16)
    wh, wl = wh_ref[...], wl_ref[...]
    logits = _nt_dot(wh, h_hi) + _nt_dot(wh, h_lo) + _nt_dot(wl, h_hi) + rb_ref[...]
    n_exp = logits.shape[0]
    e_iota = lax.broadcasted_iota(I32, logits.shape, 0)
    work = logits
    vals, idxs = [], []
    for _ in range(TOP_K):
        m = jnp.max(work, axis=0, keepdims=True)
        idx = jnp.min(jnp.where(work == m, e_iota, n_exp), axis=0, keepdims=True)
        work = jnp.where(e_iota == idx, -jnp.inf, work)
        vals.append(m)
        idxs.append(idx)
    exps = [jnp.exp(v - vals[0]) for v in vals]
    denom = exps[0] + exps[1] + exps[2] + exps[3]
    pad_rows = idx_ref.shape[0] - TOP_K
    idx_ref[...] = jnp.concatenate(idxs + [jnp.zeros((pad_rows, logits.shape[1]), I32)], axis=0)
    gate_ref[...] = jnp.concatenate([e / denom for e in exps] + [jnp.zeros((pad_rows, logits.shape[1]), F32)],
                                    axis=0)


def _router(xt, layout, mod3, nb, layer, gain, router_w, router_b):
    n_tok, d = xt.shape
    n_exp = router_w.shape[1]
    tm = layout.tile(512)
    wt = router_w.T
    w_hi = wt.astype(BF16)
    w_lo = (wt - w_hi.astype(F32)).astype(BF16)
    return pl.pallas_call(
        _router_kernel,
        grid=(layout.n_tok // tm,),
        in_specs=[pl.BlockSpec((tm, d), lambda t: (t, 0)),
                  pl.BlockSpec((1, d), lambda t: (0, 0)),
                  _mod_spec(mod3, layer, 3, nb, layout, tm),
                  _mod_spec(mod3, layer, 4, nb, layout, tm),
                  pl.BlockSpec((n_exp, d), lambda t: (0, 0)),
                  pl.BlockSpec((n_exp, d), lambda t: (0, 0)),
                  pl.BlockSpec((n_exp, 1), lambda t: (0, 0))],
        out_specs=[pl.BlockSpec((tm, d), lambda t: (t, 0)),
                   pl.BlockSpec((8, tm), lambda t: (0, t)),
                   pl.BlockSpec((8, tm), lambda t: (0, t))],
        out_shape=[jax.ShapeDtypeStruct((layout.n_tok, d), F32),
                   jax.ShapeDtypeStruct((8, layout.n_tok), I32),
                   jax.ShapeDtypeStruct((8, layout.n_tok), F32)],
        compiler_params=_params(("arbitrary",), 40),
        name="moe_router",
    )(xt, gain.reshape(1, d), mod3, mod3, w_hi, w_lo, router_b.reshape(n_exp, 1))


def _rank_kernel(idx_ref, rank_ref, cnt_ref, run_ref, *, n_exp):
    t = pl.program_id(0)

    @pl.when(t == 0)
    def _():
        run_ref[...] = jnp.zeros_like(run_ref)

    tm = idx_ref.shape[1]
    e_iota = lax.broadcasted_iota(I32, (n_exp, tm), 0)
    sel = [e_iota == idx_ref[k:k + 1, :] for k in range(TOP_K)]
    member = sel[0] | sel[1] | sel[2] | sel[3]
    member_f = jnp.where(member, 1.0, 0.0)
    strictly_before = (lax.broadcasted_iota(I32, (tm, tm), 0) < lax.broadcasted_iota(I32, (tm, tm), 1))
    prefix = jnp.dot(member_f.astype(BF16), jnp.where(strictly_before, 1.0, 0.0).astype(BF16),
                     preferred_element_type=F32)
    base = run_ref[:, 0:1] + prefix
    ranks = [jnp.sum(jnp.where(s, base, 0.0), axis=0, keepdims=True).astype(I32) for s in sel]
    pad_rows = rank_ref.shape[0] - TOP_K
    rank_ref[...] = jnp.concatenate(ranks + [jnp.zeros((pad_rows, tm), I32)], axis=0)
    run_ref[...] = run_ref[...] + jnp.sum(member_f, axis=1, keepdims=True)
    cnt_ref[...] = run_ref[...].astype(I32)


def _ranks(idx_t, n_exp):
    n_tok = idx_t.shape[1]
    tm = 512 if n_tok % 512 == 0 else 256
    return pl.pallas_call(
        functools.partial(_rank_kernel, n_exp=n_exp),
        grid=(n_tok // tm,),
        in_specs=[pl.BlockSpec((8, tm), lambda t: (0, t))],
        out_specs=[pl.BlockSpec((8, tm), lambda t: (0, t)),
                   pl.BlockSpec((n_exp, LANES), lambda t: (0, 0))],
        out_shape=[jax.ShapeDtypeStruct((8, n_tok), I32),
                   jax.ShapeDtypeStruct((n_exp, LANES), I32)],
        scratch_shapes=[pltpu.VMEM((n_exp, LANES), F32)],
        compiler_params=_params(("arbitrary",), 32),
        name="moe_rank",
    )(idx_t)


def _dest_kernel(idx_ref, rank_ref, ps_ref, dest_ref, *, n_exp):
    tm = idx_ref.shape[1]
    e_iota = lax.broadcasted_iota(I32, (n_exp, tm), 0)
    ps = ps_ref[:, 0:1]
    rows = []
    for k in range(TOP_K):
        start = jnp.sum(jnp.where(e_iota == idx_ref[k:k + 1, :], ps, 0.0), axis=0, keepdims=True)
        rows.append(start.astype(I32) + rank_ref[k:k + 1, :])
    pad_rows = dest_ref.shape[0] - TOP_K
    dest_ref[...] = jnp.concatenate(rows + [jnp.zeros((pad_rows, tm), I32)], axis=0)


def _dests(idx_t, rank_t, pad_starts):
    n_tok = idx_t.shape[1]
    n_exp = pad_starts.shape[0]
    tm = 512 if n_tok % 512 == 0 else 256
    ps = jnp.broadcast_to(pad_starts.astype(F32)[:, None], (n_exp, LANES))
    return pl.pallas_call(
        functools.partial(_dest_kernel, n_exp=n_exp),
        grid=(n_tok // tm,),
        in_specs=[pl.BlockSpec((8, tm), lambda t: (0, t)),
                  pl.BlockSpec((8, tm), lambda t: (0, t)),
                  pl.BlockSpec((n_exp, LANES), lambda t: (0, 0))],
        out_specs=pl.BlockSpec((8, tm), lambda t: (0, t)),
        out_shape=jax.ShapeDtypeStruct((8, n_tok), I32),
        compiler_params=_params(("arbitrary",), 32),
        name="moe_dest",
    )(idx_t, rank_t, ps)


def _tile_major(dest_t, tm):
    n_tok = dest_t.shape[1]
    return dest_t[:TOP_K].reshape(TOP_K, n_tok // tm, tm).transpose(1, 0, 2).reshape(-1)


def _dispatch_kernel(cnt_ref, ps_ref, h_ref, dest_ref, xs_ref, zero_ref, sem, zsem, *, n_exp, block):
    t = pl.program_id(0)
    tm = h_ref.shape[0]

    def row_copy(r, d):
        return pltpu.make_async_copy(h_ref.at[pl.ds(r, 1)], xs_ref.at[pl.ds(d, 1)], sem)

    def issue(r, carry):
        for k in range(TOP_K):
            row_copy(r, dest_ref[k * tm + r]).start()
        return carry

    lax.fori_loop(0, tm, issue, 0, unroll=ROW_DMA_UNROLL)

    @pl.when(t == 0)
    def _():
        zero_ref[...] = jnp.zeros_like(zero_ref)

        def zero_copy(d):
            return pltpu.make_async_copy(zero_ref.at[pl.ds(0, 1)], xs_ref.at[pl.ds(d, 1)], zsem)

        for e in range(n_exp):
            cnt = cnt_ref[e]
            first = ps_ref[e] + cnt
            n_pad = (-cnt) % block

            def zissue(i, carry, first=first):
                zero_copy(first + i).start()
                return carry

            def zwait(i, carry):
                zero_copy(0).wait()
                return carry

            lax.fori_loop(0, n_pad, zissue, 0)
            lax.fori_loop(0, n_pad, zwait, 0)

        def tail_copy(j):
            return pltpu.make_async_copy(zero_ref, xs_ref.at[pl.ds(j * block, block)], zsem)

        def tail_issue(j, carry):
            tail_copy(j).start()
            return carry

        def tail_wait(j, carry):
            tail_copy(j).wait()
            return carry

        first_tail = ps_ref[n_exp] // block
        lax.fori_loop(first_tail, xs_ref.shape[0] // block, tail_issue, 0)
        lax.fori_loop(first_tail, xs_ref.shape[0] // block, tail_wait, 0)

    for _ in range(TOP_K):
        pltpu.make_async_copy(h_ref, xs_ref.at[pl.ds(0, tm)], sem).wait()


def _dispatch(h, dest_flat, counts, pad_starts, n_slots, tm):
    n_tok, d = h.shape
    n_exp = counts.shape[0]
    grid_spec = pltpu.PrefetchScalarGridSpec(
        num_scalar_prefetch=2,
        grid=(n_tok // tm,),
        in_specs=[pl.BlockSpec((tm, d), lambda t, *_: (t, 0)),
                  pl.BlockSpec((TOP_K * tm,), lambda t, *_: (t,), memory_space=pltpu.SMEM)],
        out_specs=pl.BlockSpec(memory_space=pl.ANY),
        scratch_shapes=[pltpu.VMEM((MOE_BLOCK, d), F32), pltpu.SemaphoreType.DMA, pltpu.SemaphoreType.DMA],
    )
    return pl.pallas_call(
        functools.partial(_dispatch_kernel, n_exp=n_exp, block=MOE_BLOCK),
        grid_spec=grid_spec,
        out_shape=jax.ShapeDtypeStruct((n_slots, d), F32),
        compiler_params=_params(("arbitrary",), 32),
        name="moe_dispatch",
    )(counts, pad_starts, h, dest_flat)


def _expert_kernel(be_ref, nu_ref, x_ref, w1_ref, b1_ref, w2_ref, b2_ref, y_ref):
    j = pl.program_id(0)

    @pl.when(j < nu_ref[0])
    def _():
        f = w2_ref.shape[2]
        hg = jnp.dot(x_ref[...].astype(BF16), w1_ref[0, 0], preferred_element_type=F32) + b1_ref[0, 0]
        gate = jnp.minimum(hg[:, :f], SWIGLU_LIMIT)
        up = jnp.clip(hg[:, f:], -SWIGLU_LIMIT, SWIGLU_LIMIT)
        glu = gate * jax.nn.sigmoid(SWIGLU_ALPHA * gate)
        act = (glu * (up + 1.0)).astype(BF16)
        y_ref[...] = jnp.dot(act, w2_ref[0, 0], preferred_element_type=F32) + b2_ref[0, 0]

    @pl.when(j >= nu_ref[0])
    def _():
        y_ref[...] = jnp.zeros_like(y_ref)


def _experts(xs, block_expert, n_used, w1, b1, w2, b2, layer):
    n_slots, d = xs.shape
    depth, n_exp, _, f2 = w1.shape
    f = w2.shape[2]
    bm = MOE_BLOCK

    def row_block(j, be, nu):
        return (jnp.minimum(j, nu[0] - 1), 0)

    grid_spec = pltpu.PrefetchScalarGridSpec(
        num_scalar_prefetch=2,
        grid=(n_slots // bm,),
        in_specs=[pl.BlockSpec((bm, d), row_block),
                  pl.BlockSpec((1, 1, d, f2), lambda j, be, nu: (layer, be[j], 0, 0)),
                  pl.BlockSpec((1, 1, 1, f2), lambda j, be, nu: (layer, be[j], 0, 0)),
                  pl.BlockSpec((1, 1, f, d), lambda j, be, nu: (layer, be[j], 0, 0)),
                  pl.BlockSpec((1, 1, 1, d), lambda j, be, nu: (layer, be[j], 0, 0))],
        out_specs=pl.BlockSpec((bm, d), lambda j, be, nu: (j, 0)),
    )
    return pl.pallas_call(
        _expert_kernel,
        grid_spec=grid_spec,
        out_shape=jax.ShapeDtypeStruct((n_slots, d), F32),
        compiler_params=_params(("arbitrary",), 48),
        name="moe_experts",
    )(block_expert, n_used, xs, w1, b1.reshape(depth, n_exp, 1, f2), w2, b2.reshape(depth, n_exp, 1, d))


def _combine_kernel(x_ref, g_ref, dest_ref, gt_ref, fg_ref, ys_ref, o_ref, buf_ref, sem, *, final_norm):
    tm = x_ref.shape[0]

    def row_copy(k, r, d):
        return pltpu.make_async_copy(ys_ref.at[pl.ds(d, 1)], buf_ref.at[k, pl.ds(r, 1)], sem)

    def issue(r, carry):
        for k in range(TOP_K):
            row_copy(k, r, dest_ref[k * tm + r]).start()
        return carry

    lax.fori_loop(0, tm, issue, 0, unroll=ROW_DMA_UNROLL)

    g8 = g_ref[...]
    g_cols = jnp.concatenate([g8, jnp.zeros((tm - 8, tm), F32)], axis=0).T

    for k in range(TOP_K):
        pltpu.make_async_copy(ys_ref.at[pl.ds(0, tm)], buf_ref.at[k], sem).wait()

    acc = g_cols[:, 0:1] * buf_ref[0]
    for k in range(1, TOP_K):
        acc = acc + g_cols[:, k:k + 1] * buf_ref[k]
    out = x_ref[...] + gt_ref[0] * acc
    if final_norm:
        ms = jnp.mean(out * out, axis=-1, keepdims=True)
        out = out * lax.rsqrt(ms + RMS_EPS) * fg_ref[...]
    o_ref[...] = out


def _combine(xt, out_layout, gates_t, dest_flat, ys, mod3, nb, layer, final_gain, tm):
    d = xt.shape[1]
    final_norm = final_gain is not None
    fg = (final_gain if final_norm else jnp.ones((d,), F32)).reshape(1, d)
    return pl.pallas_call(
        functools.partial(_combine_kernel, final_norm=final_norm),
        grid=(out_layout.n_tok // tm,),
        in_specs=[pl.BlockSpec((tm, d), lambda t: (t, 0)),
                  pl.BlockSpec((8, tm), lambda t: (0, t)),
                  pl.BlockSpec((TOP_K * tm,), lambda t: (t,), memory_space=pltpu.SMEM),
                  _mod_spec(mod3, layer, 5, nb, out_layout, tm),
                  pl.BlockSpec((1, d), lambda t: (0, 0)),
                  pl.BlockSpec(memory_space=pl.ANY)],
        out_specs=pl.BlockSpec((tm, d), lambda t: (t, 0)),
        out_shape=jax.ShapeDtypeStruct((out_layout.n_tok, d), F32),
        scratch_shapes=[pltpu.VMEM((TOP_K, tm, d), F32), pltpu.SemaphoreType.DMA],
        compiler_params=_params(("arbitrary",), 32),
        name="moe_combine",
    )(xt, gates_t, dest_flat, mod3, fg, ys)


def _moe(xt, layout, out_layout, mod3, nb, layer, gain, router_w, router_b, w1, b1, w2, b2, final_gain):
    n_exp = router_w.shape[1]
    h, idx_t, gates_t = _router(xt, layout, mod3, nb, layer, gain, router_w, router_b)
    rank_t, cnt = _ranks(idx_t, n_exp)
    counts = cnt[:, 0]
    padded = (counts + MOE_BLOCK - 1) // MOE_BLOCK * MOE_BLOCK
    pad_ends = jnp.cumsum(padded)
    pad_starts = pad_ends - padded
    n_blocks = -(-layout.n_tok * TOP_K // MOE_BLOCK) + n_exp
    n_slots = n_blocks * MOE_BLOCK
    block_first_row = jnp.arange(n_blocks, dtype=I32) * MOE_BLOCK
    block_expert = jnp.minimum(jnp.sum(pad_ends[None, :] <= block_first_row[:, None], axis=1), n_exp - 1).astype(I32)
    n_used = (pad_ends[-1:] // MOE_BLOCK).astype(I32)
    dest_t = _dests(idx_t, rank_t, pad_starts)
    tm_d = layout.tile(256)
    tm_c = tm_d
    slot_bounds = jnp.concatenate([pad_starts, pad_ends[-1:]]).astype(I32)
    xs = _dispatch(h, _tile_major(dest_t, tm_d), counts, slot_bounds, n_slots, tm_d)
    ys = _experts(xs, block_expert, n_used, w1, b1, w2, b2, layer)
    return _combine(xt, out_layout, gates_t, _tile_major(dest_t, tm_c), ys, mod3, nb, layer, final_gain, tm_c)


def _nm_matmul_kernel(x_ref, gain_ref, sh_ref, sc_ref, w_ref, b_ref, o_ref, h_ref):
    @pl.when(pl.program_id(1) == 0)
    def _():
        h_ref[...] = _norm_mod(x_ref[...], gain_ref[...], sh_ref[0], sc_ref[0]).astype(BF16)

    o_ref[...] = (jnp.dot(h_ref[...], w_ref[...], preferred_element_type=F32) + b_ref[...]).astype(o_ref.dtype)


def _nm_matmul(xt, layout, mod3, nb, layer, gain, w, b, tm_want, tn):
    d = xt.shape[1]
    n = w.shape[1]
    tm = layout.tile(tm_want)
    return pl.pallas_call(
        _nm_matmul_kernel,
        grid=(layout.n_tok // tm, n // tn),
        in_specs=[pl.BlockSpec((tm, d), lambda m, j: (m, 0)),
                  pl.BlockSpec((1, d), lambda m, j: (0, 0)),
                  _mod_spec(mod3, layer, 0, nb, layout, tm),
                  _mod_spec(mod3, layer, 1, nb, layout, tm),
                  pl.BlockSpec((d, tn), lambda m, j: (0, j)),
                  pl.BlockSpec((1, tn), lambda m, j: (0, j))],
        out_specs=pl.BlockSpec((tm, tn), lambda m, j: (m, j)),
        out_shape=jax.ShapeDtypeStruct((layout.n_tok, n), BF16),
        scratch_shapes=[pltpu.VMEM((tm, d), BF16)],
        compiler_params=_params(("arbitrary", "arbitrary"), 48),
        name="norm_mod_matmul",
    )(xt, gain.reshape(1, d), mod3, mod3, w.astype(BF16), b.reshape(1, n))


def _proj_residual_kernel(a_ref, w_ref, b_ref, x_ref, gt_ref, o_ref):
    y = jnp.dot(a_ref[...], w_ref[...], preferred_element_type=F32) + b_ref[...]
    o_ref[...] = x_ref[...] + gt_ref[0] * y


def _proj_residual(a, w, b, xt, out_layout, mod3, nb, layer):
    k, n = w.shape
    tm = out_layout.tile(512)
    return pl.pallas_call(
        _proj_residual_kernel,
        grid=(out_layout.n_tok // tm,),
        in_specs=[pl.BlockSpec((tm, k), lambda m: (m, 0)),
                  pl.BlockSpec((k, n), lambda m: (0, 0)),
                  pl.BlockSpec((1, n), lambda m: (0, 0)),
                  pl.BlockSpec((tm, n), lambda m: (m, 0)),
                  _mod_spec(mod3, layer, 2, nb, out_layout, tm)],
        out_specs=pl.BlockSpec((tm, n), lambda m: (m, 0)),
        out_shape=jax.ShapeDtypeStruct((out_layout.n_tok, n), F32),
        compiler_params=_params(("arbitrary",), 48),
        name="proj_residual",
    )(a, w.astype(BF16), b.reshape(1, n), xt, mod3)


def _na_bias_slabs(rpb, rows):
    n_heads = rpb.shape[0]
    kr = min(NA_ROWS, rows)
    col = jnp.arange(GRID_W)
    c0 = jnp.clip(col - NA_COLS // 2, 0, GRID_W - NA_COLS)
    col_mask = (col[None, :] >= c0[:, None]) & (col[None, :] < c0[:, None] + NA_COLS)
    dc_idx = jnp.clip(col[None, :] - col[:, None] + NA_COLS - 1, 0, 2 * NA_COLS - 2)
    n_place = NA_ROWS
    dr = jnp.arange(n_place)[:, None] + jnp.arange(kr)[None, :]
    bias = rpb[:, dr][:, :, :, dc_idx]
    bias = jnp.where(col_mask[None, None, None], bias, MASK_VALUE)
    bias = jnp.transpose(bias, (1, 0, 3, 2, 4))
    return bias.reshape(n_place, n_heads // HEADS_PER_LANE_GROUP, HEADS_PER_LANE_GROUP * GRID_W, kr * GRID_W)


def _na_kernel(q_ref, k_ref, v_ref, kc_ref, vc_ref, bias_ref, o_ref, *, rows, kr, n_groups):
    r = pl.program_id(2)
    r0 = jnp.clip(r - kr // 2, 0, rows - kr)
    win = pl.ds(pl.multiple_of(r0 * GRID_W, GRID_W), kr * GRID_W)
    scale = NA_HEAD_DIM ** -0.5
    nq = HEADS_PER_LANE_GROUP * GRID_W
    head_of_row = lax.broadcasted_iota(I32, (nq, LANES), 0) // GRID_W
    head_of_lane = lax.broadcasted_iota(I32, (nq, LANES), 1) // NA_HEAD_DIM
    own = head_of_row == head_of_lane
    for g in range(n_groups):
        gs = slice(g * LANES, (g + 1) * LANES)
        qg = q_ref[:, gs]
        qbd = jnp.where(own, jnp.concatenate([qg] * HEADS_PER_LANE_GROUP, axis=0), jnp.zeros((), BF16))
        s_lat = _nt_dot(qbd, k_ref[win, gs]) * scale + bias_ref[0, g]
        s_ctx = _nt_dot(qbd, kc_ref[:, gs]) * scale
        m = jnp.maximum(jnp.max(s_lat, axis=-1, keepdims=True), jnp.max(s_ctx, axis=-1, keepdims=True))
        p_lat = jnp.exp(s_lat - m)
        p_ctx = jnp.exp(s_ctx - m)
        denom = jnp.sum(p_lat, axis=-1, keepdims=True) + jnp.sum(p_ctx, axis=-1, keepdims=True)
        o = (jnp.dot(p_lat.astype(BF16), v_ref[win, gs], preferred_element_type=F32)
             + jnp.dot(p_ctx.astype(BF16), vc_ref[:, gs], preferred_element_type=F32)) / denom
        o = jnp.where(own, o, 0.0)
        og = o[0:GRID_W]
        for h in range(1, HEADS_PER_LANE_GROUP):
            og = og + o[h * GRID_W:(h + 1) * GRID_W]
        o_ref[:, gs] = og.astype(o_ref.dtype)


def _neighbourhood_attention(qkv, bias_slabs, bsz, s_len, l_len, d):
    rows = s_len // GRID_W
    kr = min(NA_ROWS, rows)
    n_split = 2
    dh = d // n_split
    n_groups = dh // LANES
    ctx_block0 = bsz * s_len // l_len

    def place(r):
        r0 = jnp.clip(r - kr // 2, 0, rows - kr)
        return r0 - r + NA_ROWS - 1

    return pl.pallas_call(
        functools.partial(_na_kernel, rows=rows, kr=kr, n_groups=n_groups),
        grid=(bsz, n_split, rows),
        in_specs=[pl.BlockSpec((GRID_W, dh), lambda b, h, r: (b * rows + r, h)),
                  pl.BlockSpec((s_len, dh), lambda b, h, r: (b, n_split + h)),
                  pl.BlockSpec((s_len, dh), lambda b, h, r: (b, 2 * n_split + h)),
                  pl.BlockSpec((l_len, dh), lambda b, h, r: (ctx_block0 + b, n_split + h)),
                  pl.BlockSpec((l_len, dh), lambda b, h, r: (ctx_block0 + b, 2 * n_split + h)),
                  pl.BlockSpec((1, n_groups, HEADS_PER_LANE_GROUP * GRID_W, kr * GRID_W),
                               lambda b, h, r: (place(r), h, 0, 0))],
        out_specs=pl.BlockSpec((GRID_W, dh), lambda b, h, r: (b * rows + r, h)),
        out_shape=jax.ShapeDtypeStruct((bsz * s_len, d), BF16),
        compiler_params=_params(("arbitrary", "arbitrary", "arbitrary"), 48),
        name="neighbourhood_attention",
    )(qkv, qkv, qkv, qkv, qkv, bias_slabs)


def _dft_cos_sin(n):
    j = jnp.arange(n, dtype=I32)
    ang = ((j[:, None] * j[None, :]) % n).astype(F32) * (2.0 * jnp.pi / n)
    norm = n ** -0.5
    return jnp.cos(ang) * norm, jnp.sin(ang) * norm


def _fnet_stage1_kernel(x_ref, gain_ref, sh_ref, sc_ref, cs_ref, o_ref):
    h = _norm_mod(x_ref[...], gain_ref[...], sh_ref[0], sc_ref[0]).astype(BF16)
    dg = cs_ref.shape[0]
    for g in range(FNET_GROUPS):
        cols = slice(g * dg, (g + 1) * dg)
        a = jnp.dot(h[:, cols], cs_ref[...], preferred_element_type=F32)
        o_ref[0, 0, :, cols] = a[:, :dg].astype(o_ref.dtype)
        o_ref[0, 1, :, cols] = a[:, dg:].astype(o_ref.dtype)


def _fnet_stage1(xt, layout, mod3, nb, layer, gain):
    d = xt.shape[1]
    dg = d // FNET_GROUPS
    tm = layout.tile(512)
    tiles_per_seq = layout.s_len // tm
    cc, sc = _dft_cos_sin(dg)
    cs = jnp.concatenate([cc, sc], axis=1).astype(BF16)
    return pl.pallas_call(
        _fnet_stage1_kernel,
        grid=(layout.n_lat // tm,),
        in_specs=[pl.BlockSpec((tm, d), lambda t: (t, 0)),
                  pl.BlockSpec((1, d), lambda t: (0, 0)),
                  _mod_spec(mod3, layer, 0, nb, layout, tm),
                  _mod_spec(mod3, layer, 1, nb, layout, tm),
                  pl.BlockSpec((dg, 2 * dg), lambda t: (0, 0))],
        out_specs=pl.BlockSpec((1, 2, tm, d), lambda t: (t // tiles_per_seq, 0, t % tiles_per_seq, 0)),
        out_shape=jax.ShapeDtypeStruct((layout.bsz, 2, layout.s_len, d), BF16),
        compiler_params=_params(("arbitrary",), 40),
        name="fnet_channel_dft",
    )(xt, gain.reshape(1, d), mod3, mod3, cs)


def _fnet_stage2_kernel(l_ref, a_ref, o_ref):
    o_ref[...] = jnp.dot(l_ref[...], a_ref[0], preferred_element_type=F32).astype(o_ref.dtype)


def _fnet_stage2(a, s_len):
    bsz, _, _, d = a.shape
    cn, sn = _dft_cos_sin(s_len)
    lhs = jnp.concatenate([cn, -sn], axis=1).astype(BF16)
    tm = min(512, s_len)
    tn = min(1024, d)
    m_tiles = s_len // tm
    return pl.pallas_call(
        _fnet_stage2_kernel,
        grid=(bsz, d // tn, m_tiles),
        in_specs=[pl.BlockSpec((tm, 2 * s_len), lambda b, j, m: (m, 0)),
                  pl.BlockSpec((1, 2 * s_len, tn), lambda b, j, m: (b, 0, j))],
        out_specs=pl.BlockSpec((tm, tn), lambda b, j, m: (b * m_tiles + m, j)),
        out_shape=jax.ShapeDtypeStruct((bsz * s_len, d), BF16),
        compiler_params=_params(("arbitrary", "arbitrary", "arbitrary"), 48),
        name="fnet_position_dft",
    )(lhs, a.reshape(bsz, 2 * s_len, d))


def kernel(x, c, ctx, c_ctx, w_mod, b_mod, norm_gain, final_gain, pool_w, pool_scale, na_w_qkv, na_b_qkv, na_rpb, na_w_o, na_b_o, fnet_w, fnet_b, router_w, router_b, exp_w1, exp_b1, exp_w2, exp_b2):
    bsz, s_len, d = x.shape
    l_len = ctx.shape[1]
    depth = w_mod.shape[0]
    last_ctx_layer = ((depth - 2) // N_MIXERS) * N_MIXERS + 1

    nb = -(-(bsz + 1) // 8) * 8
    c_all = jnp.concatenate([c, c_ctx[None], jnp.zeros((nb - bsz - 1, d), F32)], axis=0)
    mod = _modulation(c_all, w_mod, b_mod)
    mod3 = mod.reshape(depth, nb, N_MOD, d).transpose(0, 2, 1, 3).reshape(depth * N_MOD * nb, 1, d)

    w1_bf16 = exp_w1.astype(BF16)
    w2_bf16 = exp_w2.astype(BF16)
    lat_only = _Layout(bsz, s_len, l_len, False)
    with_ctx = _Layout(bsz, s_len, l_len, True)
    xt = jnp.concatenate([x.reshape(bsz * s_len, d), ctx.reshape(bsz * l_len, d)], axis=0)
    for i in range(depth):
        kind, j = i % N_MIXERS, i // N_MIXERS
        ctx_in = i <= last_ctx_layer
        ctx_out = i < last_ctx_layer
        layout_in = with_ctx if ctx_in else lat_only
        layout_out = with_ctx if ctx_out else lat_only
        if kind == 0:
            xt = _pool_mixer(xt, layout_out, mod3, nb, i, norm_gain[i, 0], pool_w[j], pool_scale[j])
        elif kind == 1:
            if ctx_in:
                qkv = _nm_matmul(xt, layout_in, mod3, nb, i, norm_gain[i, 0], na_w_qkv[j], na_b_qkv[j], 1024,
                                 min(1024, d))
                slabs = _na_bias_slabs(na_rpb[j], s_len // GRID_W)
                att = _neighbourhood_attention(qkv, slabs, bsz, s_len, l_len, d)
                xt = _proj_residual(att, na_w_o[j], na_b_o[j], xt, layout_out, mod3, nb, i)
            else:
                raise NotImplementedError("neighbourhood attention without a live context stream")
        else:
            a = _fnet_stage1(xt, layout_out, mod3, nb, i, norm_gain[i, 0])
            f = _fnet_stage2(a, s_len)
            xt = _proj_residual(f, fnet_w[j], fnet_b[j], xt, layout_out, mod3, nb, i)
        xt = _moe(xt, layout_out, layout_out, mod3, nb, i, norm_gain[i, 1], router_w[i], router_b[i],
                  w1_bf16, exp_b1, w2_bf16, exp_b2, final_gain if i == depth - 1 else None)
    return xt[:bsz * s_len].reshape(bsz, s_len, d)
```

```python
import functools

import jax
import jax.numpy as jnp
from jax import lax
from jax.experimental import pallas as pl
from jax.experimental.pallas import tpu as pltpu

F32 = jnp.float32
BF16 = jnp.bfloat16
I32 = jnp.int32

GRID_W = 64
N_MIXERS = 3
N_MOD = 6
RMS_EPS = 1e-6
POOL_WINDOWS = (2, 4, 8, 16)
POOL_HALO = 8
NA_HEAD_DIM = 32
NA_ROWS = 8
NA_COLS = 16
FNET_GROUPS = 4
TOP_K = 4
SWIGLU_LIMIT = 7.0
SWIGLU_ALPHA = 1.702
MOE_BLOCK = 512
LOG2_E = 1.4426950408889634
MASK_VALUE = -1e30
SUBLANES = 8

LANES = 128
HEADS_PER_LANE_GROUP = LANES // NA_HEAD_DIM
MIB = 1024 * 1024


def _params(semantics, vmem_mib):
    return pltpu.CompilerParams(dimension_semantics=semantics, vmem_limit_bytes=vmem_mib * MIB)


def _norm_mod(x, gain, shift, scale):
    ms = jnp.mean(x * x, axis=-1, keepdims=True)
    y = x * lax.rsqrt(ms + RMS_EPS) * gain
    return y * (1.0 + scale) + shift


def _nt_dot(a, b):
    return lax.dot_general(a, b, (((1,), (1,)), ((), ())), preferred_element_type=F32)


def _mod_kernel(c_ref, w_ref, b_ref, o_ref):
    c = c_ref[...]
    s = (c * jax.nn.sigmoid(c)).astype(BF16)
    o_ref[0] = jnp.dot(s, w_ref[0].astype(BF16), preferred_element_type=F32) + b_ref[0]


def _modulation(c_all, w_mod, b_mod):
    depth, d, n = w_mod.shape
    nb = c_all.shape[0]
    tn = 1024
    return pl.pallas_call(
        _mod_kernel,
        grid=(depth, n // tn),
        in_specs=[pl.BlockSpec((nb, d), lambda i, j: (0, 0)),
                  pl.BlockSpec((1, d, tn), lambda i, j: (i, 0, j)),
                  pl.BlockSpec((1, 1, tn), lambda i, j: (i, 0, j))],
        out_specs=pl.BlockSpec((1, nb, tn), lambda i, j: (i, 0, j)),
        out_shape=jax.ShapeDtypeStruct((depth, nb, n), F32),
        compiler_params=_params(("arbitrary", "arbitrary"), 40),
        name="modulation",
    )(c_all, w_mod, b_mod.reshape(depth, 1, n))


class _Layout:
    def __init__(self, bsz, s_len, l_len, with_ctx):
        self.bsz, self.s_len, self.l_len, self.with_ctx = bsz, s_len, l_len, with_ctx
        self.n_lat = bsz * s_len
        self.n_tok = self.n_lat + (bsz * l_len if with_ctx else 0)

    def tile(self, want, seq_local=False):
        tm = min(want, self.s_len)
        if self.with_ctx:
            ctx_span = self.l_len if seq_local else self.bsz * self.l_len
            tm = min(tm, ctx_span)
            assert ctx_span % tm == 0
        assert self.s_len % tm == 0
        return tm

    def mod_row(self, t, tm):
        lat = (t * tm) // self.s_len
        if not self.with_ctx:
            return lat
        return jnp.where(t < self.n_lat // tm, lat, self.bsz)


def _mod_spec(mod3, layer, which, nb, layout, tm):
    base = (layer * N_MOD + which) * nb
    d = mod3.shape[-1]
    return pl.BlockSpec((1, 1, d), lambda t, *_: (base + layout.mod_row(t, tm), 0, 0))


def _pool_kernel(xc_ref, xp_ref, xn_ref, gain_ref, sh_ref, sc_ref, gt_ref, pw_ref, ps_ref, o_ref, ext_ref,
                 *, tm, s_len, l_len, n_lat_tiles, with_ctx):
    t = pl.program_id(0)
    if with_ctx:
        is_lat = t < n_lat_tiles
        tiles_per_seq = jnp.where(is_lat, s_len // tm, l_len // tm)
        tile_in_seq = jnp.where(is_lat, t % (s_len // tm), (t - n_lat_tiles) % (l_len // tm))
        seq_len = jnp.where(is_lat, s_len, l_len)
    else:
        tiles_per_seq = s_len // tm
        tile_in_seq = t % (s_len // tm)
        seq_len = s_len
    gain, shift, scale = gain_ref[...], sh_ref[0], sc_ref[0]
    hc = _norm_mod(xc_ref[...], gain, shift, scale)
    hp = _norm_mod(xp_ref[...], gain, shift, scale)
    hn = _norm_mod(xn_ref[...], gain, shift, scale)
    ext_ref[0:POOL_HALO] = jnp.where(tile_in_seq > 0, hp, 0.0)
    ext_ref[POOL_HALO:POOL_HALO + tm] = hc
    ext_ref[POOL_HALO + tm:2 * POOL_HALO + tm] = jnp.where(tile_in_seq < tiles_per_seq - 1, hn, 0.0)
    pos = tile_in_seq * tm + lax.broadcasted_iota(I32, (tm, 1), 0)
    dg = hc.shape[1] // len(POOL_WINDOWS)
    gate, pscale = gt_ref[0], ps_ref[...]
    for g, w in enumerate(POOL_WINDOWS):
        cs = slice(g * dg, (g + 1) * dg)
        start = POOL_HALO - w // 2
        acc = ext_ref[start:start + tm, cs]
        for j in range(1, w):
            acc = acc + ext_ref[start + j:start + j + tm, cs]
        cnt = jnp.minimum(pos + w // 2, seq_len) - jnp.maximum(pos - w // 2, 0)
        diff = (acc / cnt.astype(F32) - hc[:, cs]).astype(BF16)
        y = jnp.dot(diff, pw_ref[g], preferred_element_type=F32)
        o_ref[:, cs] = xc_ref[:, cs] + gate[:, cs] * (y * pscale[:, cs])


def _pool_mixer(xt, layout, mod3, nb, layer, gain, pool_w, pool_scale):
    n_tok, d = xt.shape
    tm = layout.tile(256, seq_local=True)
    hb = tm // POOL_HALO
    n_halo_blocks = n_tok // POOL_HALO
    kern = functools.partial(_pool_kernel, tm=tm, s_len=layout.s_len, l_len=layout.l_len,
                             n_lat_tiles=layout.n_lat // tm, with_ctx=layout.with_ctx)
    return pl.pallas_call(
        kern,
        grid=(layout.n_tok // tm,),
        in_specs=[pl.BlockSpec((tm, d), lambda t: (t, 0)),
                  pl.BlockSpec((POOL_HALO, d), lambda t: (jnp.maximum(t * hb - 1, 0), 0)),
                  pl.BlockSpec((POOL_HALO, d), lambda t: (jnp.minimum((t + 1) * hb, n_halo_blocks - 1), 0)),
                  pl.BlockSpec((1, d), lambda t: (0, 0)),
                  _mod_spec(mod3, layer, 0, nb, layout, tm),
                  _mod_spec(mod3, layer, 1, nb, layout, tm),
                  _mod_spec(mod3, layer, 2, nb, layout, tm),
                  pl.BlockSpec(pool_w.shape, lambda t: (0, 0, 0)),
                  pl.BlockSpec((1, d), lambda t: (0, 0))],
        out_specs=pl.BlockSpec((tm, d), lambda t: (t, 0)),
        out_shape=jax.ShapeDtypeStruct((layout.n_tok, d), F32),
        scratch_shapes=[pltpu.VMEM((tm + 2 * POOL_HALO, d), F32)],
        compiler_params=_params(("arbitrary",), 40),
        name="pool_mixer",
    )(xt, xt, xt, gain.reshape(1, d), mod3, mod3, mod3, pool_w.astype(BF16), pool_scale.reshape(1, d))


def _router_kernel(x_ref, gain_ref, sh_ref, sc_ref, wh_ref, wl_ref, rb_ref, h_ref, idx_ref, gate_ref):
    h = _norm_mod(x_ref[...], gain_ref[...], sh_ref[0], sc_ref[0])
    h_ref[...] = h
    h_hi = h.astype(BF16)
    h_lo = (h - h_hi.astype(F32)).astype(BF16)
    wh, wl = wh_ref[...], wl_ref[...]
    logits = _nt_dot(wh, h_hi) + _nt_dot(wh, h_lo) + _nt_dot(wl, h_hi) + rb_ref[...]
    n_exp = logits.shape[0]
    e_iota = lax.broadcasted_iota(I32, logits.shape, 0)
    work = logits
    vals, idxs = [], []
    for _ in range(TOP_K):
        m = jnp.max(work, axis=0, keepdims=True)
        idx = jnp.min(jnp.where(work == m, e_iota, n_exp), axis=0, keepdims=True)
        work = jnp.where(e_iota == idx, -jnp.inf, work)
        vals.append(m)
        idxs.append(idx)
    exps = [jnp.exp(v - vals[0]) for v in vals]
    denom = exps[0] + exps[1] + exps[2] + exps[3]
    pad_rows = idx_ref.shape[0] - TOP_K
    idx_ref[...] = jnp.concatenate(idxs + [jnp.zeros((pad_rows, logits.shape[1]), I32)], axis=0)
    gate_ref[...] = jnp.concatenate([e / denom for e in exps] + [jnp.zeros((pad_rows, logits.shape[1]), F32)],
                                    axis=0)


def _router(xt, layout, mod3, nb, layer, gain, router_w, router_b):
    n_tok, d = xt.shape
    n_exp = router_w.shape[1]
    tm = layout.tile(512)
    wt = router_w.T
    w_hi = wt.astype(BF16)
    w_lo = (wt - w_hi.astype(F32)).astype(BF16)
    return pl.pallas_call(
        _router_kernel,
        grid=(layout.n_tok // tm,),
        in_specs=[pl.BlockSpec((tm, d), lambda t: (t, 0)),
                  pl.BlockSpec((1, d), lambda t: (0, 0)),
                  _mod_spec(mod3, layer, 3, nb, layout, tm),
                  _mod_spec(mod3, layer, 4, nb, layout, tm),
                  pl.BlockSpec((n_exp, d), lambda t: (0, 0)),
                  pl.BlockSpec((n_exp, d), lambda t: (0, 0)),
                  pl.BlockSpec((n_exp, 1), lambda t: (0, 0))],
        out_specs=[pl.BlockSpec((tm, d), lambda t: (t, 0)),
                   pl.BlockSpec((8, tm), lambda t: (0, t)),
                   pl.BlockSpec((8, tm), lambda t: (0, t))],
        out_shape=[jax.ShapeDtypeStruct((layout.n_tok, d), F32),
                   jax.ShapeDtypeStruct((8, layout.n_tok), I32),
                   jax.ShapeDtypeStruct((8, layout.n_tok), F32)],
        compiler_params=_params(("arbitrary",), 40),
        name="moe_router",
    )(xt, gain.reshape(1, d), mod3, mod3, w_hi, w_lo, router_b.reshape(n_exp, 1))


def _rank_kernel(idx_ref, rank_ref, cnt_ref, run_ref, *, n_exp):
    t = pl.program_id(0)

    @pl.when(t == 0)
    def _():
        run_ref[...] = jnp.zeros_like(run_ref)

    tm = idx_ref.shape[1]
    e_iota = lax.broadcasted_iota(I32, (n_exp, tm), 0)
    sel = [e_iota == idx_ref[k:k + 1, :] for k in range(TOP_K)]
    member = sel[0] | sel[1] | sel[2] | sel[3]
    member_f = jnp.where(member, 1.0, 0.0)
    strictly_before = (lax.broadcasted_iota(I32, (tm, tm), 0) < lax.broadcasted_iota(I32, (tm, tm), 1))
    prefix = jnp.dot(member_f.astype(BF16), jnp.where(strictly_before, 1.0, 0.0).astype(BF16),
                     preferred_element_type=F32)
    base = run_ref[:, 0:1] + prefix
    ranks = [jnp.sum(jnp.where(s, base, 0.0), axis=0, keepdims=True).astype(I32) for s in sel]
    pad_rows = rank_ref.shape[0] - TOP_K
    rank_ref[...] = jnp.concatenate(ranks + [jnp.zeros((pad_rows, tm), I32)], axis=0)
    run_ref[...] = run_ref[...] + jnp.sum(member_f, axis=1, keepdims=True)
    cnt_ref[...] = run_ref[...].astype(I32)


def _ranks(idx_t, n_exp):
    n_tok = idx_t.shape[1]
    tm = 512 if n_tok % 512 == 0 else 256
    return pl.pallas_call(
        functools.partial(_rank_kernel, n_exp=n_exp),
        grid=(n_tok // tm,),
        in_specs=[pl.BlockSpec((8, tm), lambda t: (0, t))],
        out_specs=[pl.BlockSpec((8, tm), lambda t: (0, t)),
                   pl.BlockSpec((n_exp, LANES), lambda t: (0, 0))],
        out_shape=[jax.ShapeDtypeStruct((8, n_tok), I32),
                   jax.ShapeDtypeStruct((n_exp, LANES), I32)],
        scratch_shapes=[pltpu.VMEM((n_exp, LANES), F32)],
        compiler_params=_params(("arbitrary",), 32),
        name="moe_rank",
    )(idx_t)


def _dest_kernel(idx_ref, rank_ref, ps_ref, dest_ref, *, n_exp):
    tm = idx_ref.shape[1]
    e_iota = lax.broadcasted_iota(I32, (n_exp, tm), 0)
    ps = ps_ref[:, 0:1]
    rows = []
    for k in range(TOP_K):
        start = jnp.sum(jnp.where(e_iota == idx_ref[k:k + 1, :], ps, 0.0), axis=0, keepdims=True)
        rows.append(start.astype(I32) + rank_ref[k:k + 1, :])
    pad_rows = dest_ref.shape[0] - TOP_K
    dest_ref[...] = jnp.concatenate(rows + [jnp.zeros((pad_rows, tm), I32)], axis=0)


def _dests(idx_t, rank_t, pad_starts):
    n_tok = idx_t.shape[1]
    n_exp = pad_starts.shape[0]
    tm = 512 if n_tok % 512 == 0 else 256
    ps = jnp.broadcast_to(pad_starts.astype(F32)[:, None], (n_exp, LANES))
    return pl.pallas_call(
        functools.partial(_dest_kernel, n_exp=n_exp),
        grid=(n_tok // tm,),
        in_specs=[pl.BlockSpec((8, tm), lambda t: (0, t)),
                  pl.BlockSpec((8, tm), lambda t: (0, t)),
                  pl.BlockSpec((n_exp, LANES), lambda t: (0, 0))],
        out_specs=pl.BlockSpec((8, tm), lambda t: (0, t)),
        out_shape=jax.ShapeDtypeStruct((8, n_tok), I32),
        compiler_params=_params(("arbitrary",), 32),
        name="moe_dest",
    )(idx_t, rank_t, ps)


def _tile_major(dest_t, tm):
    n_tok = dest_t.shape[1]
    return dest_t[:TOP_K].reshape(TOP_K, n_tok // tm, tm).transpose(1, 0, 2).reshape(-1)


def _dispatch_kernel(cnt_ref, ps_ref, h_ref, dest_ref, xs_ref, zero_ref, sem, zsem, *, n_exp, block):
    t = pl.program_id(0)
    tm = h_ref.shape[0] * SUBLANES

    def row_copy(r8, u, d):
        return pltpu.make_async_copy(h_ref.at[r8, pl.ds(u, 1)], xs_ref.at[pl.ds(d, 1)], sem)

    def issue(r8, carry):
        for u in range(SUBLANES):
            for k in range(TOP_K):
                row_copy(r8, u, dest_ref[k * tm + r8 * SUBLANES + u]).start()
        return carry

    lax.fori_loop(0, tm // SUBLANES, issue, 0)

    @pl.when(t == 0)
    def _():
        zero_ref[...] = jnp.zeros_like(zero_ref)

        def zero_copy(d):
            return pltpu.make_async_copy(zero_ref.at[pl.ds(0, 1)], xs_ref.at[pl.ds(d, 1)], zsem)

        for e in range(n_exp):
            cnt = cnt_ref[e]
            first = ps_ref[e] + cnt
            n_pad = (-cnt) % block

            def zissue(i, carry, first=first):
                zero_copy(first + i).start()
                return carry

            def zwait(i, carry):
                zero_copy(0).wait()
                return carry

            lax.fori_loop(0, n_pad, zissue, 0)
            lax.fori_loop(0, n_pad, zwait, 0)

        def tail_copy(j):
            return pltpu.make_async_copy(zero_ref, xs_ref.at[pl.ds(j * block, block)], zsem)

        def tail_issue(j, carry):
            tail_copy(j).start()
            return carry

        def tail_wait(j, carry):
            tail_copy(j).wait()
            return carry

        first_tail = ps_ref[n_exp] // block
        lax.fori_loop(first_tail, xs_ref.shape[0] // block, tail_issue, 0)
        lax.fori_loop(first_tail, xs_ref.shape[0] // block, tail_wait, 0)

    for _ in range(TOP_K):
        pltpu.make_async_copy(zero_ref.at[pl.ds(0, tm)], xs_ref.at[pl.ds(0, tm)], sem).wait()


def _dispatch(h, dest_flat, counts, pad_starts, n_slots, tm):
    n_tok, d = h.shape
    n_exp = counts.shape[0]
    grid_spec = pltpu.PrefetchScalarGridSpec(
        num_scalar_prefetch=2,
        grid=(n_tok // tm,),
        in_specs=[pl.BlockSpec((tm // SUBLANES, SUBLANES, d), lambda t, *_: (t, 0, 0)),
                  pl.BlockSpec((TOP_K * tm,), lambda t, *_: (t,), memory_space=pltpu.SMEM)],
        out_specs=pl.BlockSpec(memory_space=pl.ANY),
        scratch_shapes=[pltpu.VMEM((MOE_BLOCK, d), F32), pltpu.SemaphoreType.DMA, pltpu.SemaphoreType.DMA],
    )
    return pl.pallas_call(
        functools.partial(_dispatch_kernel, n_exp=n_exp, block=MOE_BLOCK),
        grid_spec=grid_spec,
        out_shape=jax.ShapeDtypeStruct((n_slots, d), F32),
        compiler_params=_params(("arbitrary",), 32),
        name="moe_dispatch",
    )(counts, pad_starts, h.reshape(n_tok // SUBLANES, SUBLANES, d), dest_flat)


def _expert_kernel(be_ref, nu_ref, x_ref, w1_ref, b1_ref, w2_ref, b2_ref, y_ref):
    j = pl.program_id(0)

    @pl.when(j < nu_ref[0])
    def _():
        f = w2_ref.shape[2]
        hg = jnp.dot(x_ref[...].astype(BF16), w1_ref[0, 0], preferred_element_type=F32) + b1_ref[0, 0]
        gate = jnp.minimum(hg[:, :f], SWIGLU_LIMIT)
        up = jnp.clip(hg[:, f:], -SWIGLU_LIMIT, SWIGLU_LIMIT)
        glu = gate * jax.nn.sigmoid(SWIGLU_ALPHA * gate)
        act = (glu * (up + 1.0)).astype(BF16)
        y_ref[...] = jnp.dot(act, w2_ref[0, 0], preferred_element_type=F32) + b2_ref[0, 0]

    @pl.when(j >= nu_ref[0])
    def _():
        y_ref[...] = jnp.zeros_like(y_ref)


def _experts(xs, block_expert, n_used, w1, b1, w2, b2, layer):
    n_slots, d = xs.shape
    depth, n_exp, _, f2 = w1.shape
    f = w2.shape[2]
    bm = MOE_BLOCK

    def row_block(j, be, nu):
        return (jnp.minimum(j, nu[0] - 1), 0)

    grid_spec = pltpu.PrefetchScalarGridSpec(
        num_scalar_prefetch=2,
        grid=(n_slots // bm,),
        in_specs=[pl.BlockSpec((bm, d), row_block),
                  pl.BlockSpec((1, 1, d, f2), lambda j, be, nu: (layer, be[j], 0, 0)),
                  pl.BlockSpec((1, 1, 1, f2), lambda j, be, nu: (layer, be[j], 0, 0)),
                  pl.BlockSpec((1, 1, f, d), lambda j, be, nu: (layer, be[j], 0, 0)),
                  pl.BlockSpec((1, 1, 1, d), lambda j, be, nu: (layer, be[j], 0, 0))],
        out_specs=pl.BlockSpec((bm, d), lambda j, be, nu: (j, 0)),
    )
    return pl.pallas_call(
        _expert_kernel,
        grid_spec=grid_spec,
        out_shape=jax.ShapeDtypeStruct((n_slots, d), F32),
        compiler_params=_params(("arbitrary",), 56),
        name="moe_experts",
    )(block_expert, n_used, xs, w1, b1.reshape(depth, n_exp, 1, f2), w2, b2.reshape(depth, n_exp, 1, d))


def _combine_kernel(x_ref, g_ref, dest_ref, gt_ref, fg_ref, ys_ref, o_ref, buf_ref, sem, *, final_norm):
    tm = x_ref.shape[0]

    def row_copy(k, r8, u, d):
        return pltpu.make_async_copy(ys_ref.at[pl.ds(d, 1)], buf_ref.at[k, r8, pl.ds(u, 1)], sem)

    def issue(r8, carry):
        for u in range(SUBLANES):
            for k in range(TOP_K):
                row_copy(k, r8, u, dest_ref[k * tm + r8 * SUBLANES + u]).start()
        return carry

    lax.fori_loop(0, tm // SUBLANES, issue, 0)

    g8 = g_ref[...]
    g_cols = jnp.concatenate([g8, jnp.zeros((tm - 8, tm), F32)], axis=0).T

    for k in range(TOP_K):
        pltpu.make_async_copy(ys_ref.at[pl.ds(0, tm)], o_ref, sem).wait()

    d = x_ref.shape[1]
    acc = g_cols[:, 0:1] * buf_ref[0].reshape(tm, d)
    for k in range(1, TOP_K):
        acc = acc + g_cols[:, k:k + 1] * buf_ref[k].reshape(tm, d)
    out = x_ref[...] + gt_ref[0] * acc
    if final_norm:
        ms = jnp.mean(out * out, axis=-1, keepdims=True)
        out = out * lax.rsqrt(ms + RMS_EPS) * fg_ref[...]
    o_ref[...] = out


def _combine(xt, out_layout, gates_t, dest_flat, ys, mod3, nb, layer, final_gain, tm):
    d = xt.shape[1]
    final_norm = final_gain is not None
    fg = (final_gain if final_norm else jnp.ones((d,), F32)).reshape(1, d)
    return pl.pallas_call(
        functools.partial(_combine_kernel, final_norm=final_norm),
        grid=(out_layout.n_tok // tm,),
        in_specs=[pl.BlockSpec((tm, d), lambda t: (t, 0)),
                  pl.BlockSpec((8, tm), lambda t: (0, t)),
                  pl.BlockSpec((TOP_K * tm,), lambda t: (t,), memory_space=pltpu.SMEM),
                  _mod_spec(mod3, layer, 5, nb, out_layout, tm),
                  pl.BlockSpec((1, d), lambda t: (0, 0)),
                  pl.BlockSpec(memory_space=pl.ANY)],
        out_specs=pl.BlockSpec((tm, d), lambda t: (t, 0)),
        out_shape=jax.ShapeDtypeStruct((out_layout.n_tok, d), F32),
        scratch_shapes=[pltpu.VMEM((TOP_K, tm // SUBLANES, SUBLANES, d), F32), pltpu.SemaphoreType.DMA],
        compiler_params=_params(("arbitrary",), 32),
        name="moe_combine",
    )(xt, gates_t, dest_flat, mod3, fg, ys)


def _moe(xt, layout, out_layout, mod3, nb, layer, gain, router_w, router_b, w1, b1, w2, b2, final_gain):
    n_exp = router_w.shape[1]
    h, idx_t, gates_t = _router(xt, layout, mod3, nb, layer, gain, router_w, router_b)
    rank_t, cnt = _ranks(idx_t, n_exp)
    counts = cnt[:, 0]
    padded = (counts + MOE_BLOCK - 1) // MOE_BLOCK * MOE_BLOCK
    pad_ends = jnp.cumsum(padded)
    pad_starts = pad_ends - padded
    n_blocks = -(-layout.n_tok * TOP_K // MOE_BLOCK) + n_exp
    n_slots = n_blocks * MOE_BLOCK
    block_first_row = jnp.arange(n_blocks, dtype=I32) * MOE_BLOCK
    block_expert = jnp.minimum(jnp.sum(pad_ends[None, :] <= block_first_row[:, None], axis=1), n_exp - 1).astype(I32)
    n_used = (pad_ends[-1:] // MOE_BLOCK).astype(I32)
    dest_t = _dests(idx_t, rank_t, pad_starts)
    tm_d = layout.tile(256)
    tm_c = tm_d
    slot_bounds = jnp.concatenate([pad_starts, pad_ends[-1:]]).astype(I32)
    xs = _dispatch(h, _tile_major(dest_t, tm_d), counts, slot_bounds, n_slots, tm_d)
    ys = _experts(xs, block_expert, n_used, w1, b1, w2, b2, layer)
    return _combine(xt, out_layout, gates_t, _tile_major(dest_t, tm_c), ys, mod3, nb, layer, final_gain, tm_c)


def _nm_matmul_kernel(x_ref, gain_ref, sh_ref, sc_ref, w_ref, b_ref, o_ref, h_ref):
    @pl.when(pl.program_id(1) == 0)
    def _():
        h_ref[...] = _norm_mod(x_ref[...], gain_ref[...], sh_ref[0], sc_ref[0]).astype(BF16)

    o_ref[...] = (jnp.dot(h_ref[...], w_ref[...], preferred_element_type=F32) + b_ref[...]).astype(o_ref.dtype)


def _nm_matmul(xt, layout, mod3, nb, layer, gain, w, b, tm_want, tn):
    d = xt.shape[1]
    n = w.shape[1]
    tm = layout.tile(tm_want)
    return pl.pallas_call(
        _nm_matmul_kernel,
        grid=(layout.n_tok // tm, n // tn),
        in_specs=[pl.BlockSpec((tm, d), lambda m, j: (m, 0)),
                  pl.BlockSpec((1, d), lambda m, j: (0, 0)),
                  _mod_spec(mod3, layer, 0, nb, layout, tm),
                  _mod_spec(mod3, layer, 1, nb, layout, tm),
                  pl.BlockSpec((d, tn), lambda m, j: (0, j)),
                  pl.BlockSpec((1, tn), lambda m, j: (0, j))],
        out_specs=pl.BlockSpec((tm, tn), lambda m, j: (m, j)),
        out_shape=jax.ShapeDtypeStruct((layout.n_tok, n), BF16),
        scratch_shapes=[pltpu.VMEM((tm, d), BF16)],
        compiler_params=_params(("arbitrary", "arbitrary"), 48),
        name="norm_mod_matmul",
    )(xt, gain.reshape(1, d), mod3, mod3, w.astype(BF16), b.reshape(1, n))


def _proj_residual_kernel(a_ref, w_ref, b_ref, x_ref, gt_ref, o_ref):
    y = jnp.dot(a_ref[...], w_ref[...], preferred_element_type=F32) + b_ref[...]
    o_ref[...] = x_ref[...] + gt_ref[0] * y


def _proj_residual(a, w, b, xt, out_layout, mod3, nb, layer):
    k, n = w.shape
    tm = out_layout.tile(512)
    return pl.pallas_call(
        _proj_residual_kernel,
        grid=(out_layout.n_tok // tm,),
        in_specs=[pl.BlockSpec((tm, k), lambda m: (m, 0)),
                  pl.BlockSpec((k, n), lambda m: (0, 0)),
                  pl.BlockSpec((1, n), lambda m: (0, 0)),
                  pl.BlockSpec((tm, n), lambda m: (m, 0)),
                  _mod_spec(mod3, layer, 2, nb, out_layout, tm)],
        out_specs=pl.BlockSpec((tm, n), lambda m: (m, 0)),
        out_shape=jax.ShapeDtypeStruct((out_layout.n_tok, n), F32),
        compiler_params=_params(("arbitrary",), 48),
        name="proj_residual",
    )(a, w.astype(BF16), b.reshape(1, n), xt, mod3)


def _na_bias_slabs(rpb, rows):
    n_heads = rpb.shape[0]
    kr = min(NA_ROWS, rows)
    col = jnp.arange(GRID_W)
    c0 = jnp.clip(col - NA_COLS // 2, 0, GRID_W - NA_COLS)
    col_mask = (col[None, :] >= c0[:, None]) & (col[None, :] < c0[:, None] + NA_COLS)
    dc_idx = jnp.clip(col[None, :] - col[:, None] + NA_COLS - 1, 0, 2 * NA_COLS - 2)
    n_place = NA_ROWS
    dr = jnp.arange(n_place)[:, None] + jnp.arange(kr)[None, :]
    bias = rpb[:, dr][:, :, :, dc_idx]
    bias = jnp.where(col_mask[None, None, None], bias * LOG2_E, MASK_VALUE)
    bias = jnp.transpose(bias, (1, 0, 3, 2, 4))
    return bias.reshape(n_place, n_heads // HEADS_PER_LANE_GROUP, HEADS_PER_LANE_GROUP * GRID_W, kr * GRID_W)


def _na_kernel(q_ref, k_ref, v_ref, kc_ref, vc_ref, bias_ref, o_ref, *, rows, kr, n_groups):
    r = pl.program_id(2)
    r0 = jnp.clip(r - kr // 2, 0, rows - kr)
    win = pl.ds(pl.multiple_of(r0 * GRID_W, GRID_W), kr * GRID_W)
    nq = HEADS_PER_LANE_GROUP * GRID_W
    head_of_row = lax.broadcasted_iota(I32, (nq, LANES), 0) // GRID_W
    head_of_lane = lax.broadcasted_iota(I32, (nq, LANES), 1) // NA_HEAD_DIM
    own = head_of_row == head_of_lane
    for g in range(n_groups):
        gs = slice(g * LANES, (g + 1) * LANES)
        qg = q_ref[:, gs]
        qbd = jnp.where(own, jnp.concatenate([qg] * HEADS_PER_LANE_GROUP, axis=0), jnp.zeros((), BF16))
        s_lat = _nt_dot(qbd, k_ref[win, gs]) + bias_ref[0, g]
        s_ctx = _nt_dot(qbd, kc_ref[:, gs])
        m = jnp.maximum(jnp.max(s_lat, axis=-1, keepdims=True), jnp.max(s_ctx, axis=-1, keepdims=True))
        p_lat = jnp.exp2(s_lat - m)
        p_ctx = jnp.exp2(s_ctx - m)
        denom = jnp.sum(p_lat, axis=-1, keepdims=True) + jnp.sum(p_ctx, axis=-1, keepdims=True)
        o = (jnp.dot(p_lat.astype(BF16), v_ref[win, gs], preferred_element_type=F32)
             + jnp.dot(p_ctx.astype(BF16), vc_ref[:, gs], preferred_element_type=F32)) / denom
        o = jnp.where(own, o, 0.0)
        og = o[0:GRID_W]
        for h in range(1, HEADS_PER_LANE_GROUP):
            og = og + o[h * GRID_W:(h + 1) * GRID_W]
        o_ref[:, gs] = og.astype(o_ref.dtype)


def _neighbourhood_attention(qkv, bias_slabs, bsz, s_len, l_len, d):
    rows = s_len // GRID_W
    kr = min(NA_ROWS, rows)
    n_split = 2
    dh = d // n_split
    n_groups = dh // LANES
    ctx_block0 = bsz * s_len // l_len

    def place(r):
        r0 = jnp.clip(r - kr // 2, 0, rows - kr)
        return r0 - r + NA_ROWS - 1

    return pl.pallas_call(
        functools.partial(_na_kernel, rows=rows, kr=kr, n_groups=n_groups),
        grid=(bsz, n_split, rows),
        in_specs=[pl.BlockSpec((GRID_W, dh), lambda b, h, r: (b * rows + r, h)),
                  pl.BlockSpec((s_len, dh), lambda b, h, r: (b, n_split + h)),
                  pl.BlockSpec((s_len, dh), lambda b, h, r: (b, 2 * n_split + h)),
                  pl.BlockSpec((l_len, dh), lambda b, h, r: (ctx_block0 + b, n_split + h)),
                  pl.BlockSpec((l_len, dh), lambda b, h, r: (ctx_block0 + b, 2 * n_split + h)),
                  pl.BlockSpec((1, n_groups, HEADS_PER_LANE_GROUP * GRID_W, kr * GRID_W),
                               lambda b, h, r: (place(r), h, 0, 0))],
        out_specs=pl.BlockSpec((GRID_W, dh), lambda b, h, r: (b * rows + r, h)),
        out_shape=jax.ShapeDtypeStruct((bsz * s_len, d), BF16),
        compiler_params=_params(("arbitrary", "arbitrary", "arbitrary"), 48),
        name="neighbourhood_attention",
    )(qkv, qkv, qkv, qkv, qkv, bias_slabs)


def _dft_cos_sin(n):
    j = jnp.arange(n, dtype=I32)
    ang = ((j[:, None] * j[None, :]) % n).astype(F32) * (2.0 * jnp.pi / n)
    norm = n ** -0.5
    return jnp.cos(ang) * norm, jnp.sin(ang) * norm


def _fnet_stage1_kernel(x_ref, gain_ref, sh_ref, sc_ref, cs_ref, o_ref):
    h = _norm_mod(x_ref[...], gain_ref[...], sh_ref[0], sc_ref[0]).astype(BF16)
    dg = cs_ref.shape[0]
    for g in range(FNET_GROUPS):
        cols = slice(g * dg, (g + 1) * dg)
        a = jnp.dot(h[:, cols], cs_ref[...], preferred_element_type=F32)
        o_ref[0, 0, :, cols] = a[:, :dg].astype(o_ref.dtype)
        o_ref[0, 1, :, cols] = a[:, dg:].astype(o_ref.dtype)


def _fnet_stage1(xt, layout, mod3, nb, layer, gain):
    d = xt.shape[1]
    dg = d // FNET_GROUPS
    tm = layout.tile(512)
    tiles_per_seq = layout.s_len // tm
    cc, sc = _dft_cos_sin(dg)
    cs = jnp.concatenate([cc, sc], axis=1).astype(BF16)
    return pl.pallas_call(
        _fnet_stage1_kernel,
        grid=(layout.n_lat // tm,),
        in_specs=[pl.BlockSpec((tm, d), lambda t: (t, 0)),
                  pl.BlockSpec((1, d), lambda t: (0, 0)),
                  _mod_spec(mod3, layer, 0, nb, layout, tm),
                  _mod_spec(mod3, layer, 1, nb, layout, tm),
                  pl.BlockSpec((dg, 2 * dg), lambda t: (0, 0))],
        out_specs=pl.BlockSpec((1, 2, tm, d), lambda t: (t // tiles_per_seq, 0, t % tiles_per_seq, 0)),
        out_shape=jax.ShapeDtypeStruct((layout.bsz, 2, layout.s_len, d), BF16),
        compiler_params=_params(("arbitrary",), 40),
        name="fnet_channel_dft",
    )(xt, gain.reshape(1, d), mod3, mod3, cs)


def _fnet_stage2_kernel(l_ref, a_ref, o_ref):
    o_ref[...] = jnp.dot(l_ref[...], a_ref[0], preferred_element_type=F32).astype(o_ref.dtype)


def _fnet_stage2(a, s_len):
    bsz, _, _, d = a.shape
    cn, sn = _dft_cos_sin(s_len)
    lhs = jnp.concatenate([cn, -sn], axis=1).astype(BF16)
    tm = min(512, s_len)
    tn = min(1024, d)
    m_tiles = s_len // tm
    return pl.pallas_call(
        _fnet_stage2_kernel,
        grid=(bsz, d // tn, m_tiles),
        in_specs=[pl.BlockSpec((tm, 2 * s_len), lambda b, j, m: (m, 0)),
                  pl.BlockSpec((1, 2 * s_len, tn), lambda b, j, m: (b, 0, j))],
        out_specs=pl.BlockSpec((tm, tn), lambda b, j, m: (b * m_tiles + m, j)),
        out_shape=jax.ShapeDtypeStruct((bsz * s_len, d), BF16),
        compiler_params=_params(("arbitrary", "arbitrary", "arbitrary"), 48),
        name="fnet_position_dft",
    )(lhs, a.reshape(bsz, 2 * s_len, d))


def kernel(x, c, ctx, c_ctx, w_mod, b_mod, norm_gain, final_gain, pool_w, pool_scale, na_w_qkv, na_b_qkv, na_rpb, na_w_o, na_b_o, fnet_w, fnet_b, router_w, router_b, exp_w1, exp_b1, exp_w2, exp_b2):
    bsz, s_len, d = x.shape
    l_len = ctx.shape[1]
    depth = w_mod.shape[0]
    last_ctx_layer = ((depth - 2) // N_MIXERS) * N_MIXERS + 1

    nb = -(-(bsz + 1) // 8) * 8
    c_all = jnp.concatenate([c, c_ctx[None], jnp.zeros((nb - bsz - 1, d), F32)], axis=0)
    mod = _modulation(c_all, w_mod, b_mod)
    mod3 = mod.reshape(depth, nb, N_MOD, d).transpose(0, 2, 1, 3).reshape(depth * N_MOD * nb, 1, d)

    w1_bf16 = exp_w1.astype(BF16)
    w2_bf16 = exp_w2.astype(BF16)
    lat_only = _Layout(bsz, s_len, l_len, False)
    with_ctx = _Layout(bsz, s_len, l_len, True)
    xt = jnp.concatenate([x.reshape(bsz * s_len, d), ctx.reshape(bsz * l_len, d)], axis=0)
    for i in range(depth):
        kind, j = i % N_MIXERS, i // N_MIXERS
        ctx_in = i <= last_ctx_layer
        ctx_out = i < last_ctx_layer
        layout_in = with_ctx if ctx_in else lat_only
        layout_out = with_ctx if ctx_out else lat_only
        if kind == 0:
            xt = _pool_mixer(xt, layout_out, mod3, nb, i, norm_gain[i, 0], pool_w[j], pool_scale[j])
        elif kind == 1:
            if ctx_in:
                k_fold = jnp.concatenate([jnp.ones((d,), F32), jnp.full((d,), NA_HEAD_DIM ** -0.5 * LOG2_E, F32),
                                          jnp.ones((d,), F32)])
                qkv = _nm_matmul(xt, layout_in, mod3, nb, i, norm_gain[i, 0], na_w_qkv[j] * k_fold,
                                 na_b_qkv[j] * k_fold, 1024, min(1024, d))
                slabs = _na_bias_slabs(na_rpb[j], s_len // GRID_W)
                att = _neighbourhood_attention(qkv, slabs, bsz, s_len, l_len, d)
                xt = _proj_residual(att, na_w_o[j], na_b_o[j], xt, layout_out, mod3, nb, i)
            else:
                raise NotImplementedError("neighbourhood attention without a live context stream")
        else:
            a = _fnet_stage1(xt, layout_out, mod3, nb, i, norm_gain[i, 0])
            f = _fnet_stage2(a, s_len)
            xt = _proj_residual(f, fnet_w[j], fnet_b[j], xt, layout_out, mod3, nb, i)
        xt = _moe(xt, layout_out, layout_out, mod3, nb, i, norm_gain[i, 1], router_w[i], router_b[i],
                  w1_bf16, exp_b1, w2_bf16, exp_b2, final_gain if i == depth - 1 else None)
    return xt[:bsz * s_len].reshape(bsz, s_len, d)
```

```python
import functools

import jax
import jax.numpy as jnp
from jax import lax
from jax.experimental import pallas as pl
from jax.experimental.pallas import tpu as pltpu

F32 = jnp.float32
BF16 = jnp.bfloat16
I32 = jnp.int32

GRID_W = 64
N_MIXERS = 3
N_MOD = 6
RMS_EPS = 1e-6
POOL_WINDOWS = (2, 4, 8, 16)
POOL_HALO = 8
NA_HEAD_DIM = 32
NA_ROWS = 8
NA_COLS = 16
NA_KEY_CHUNK = 256
FNET_GROUPS = 4
TOP_K = 4
SWIGLU_LIMIT = 7.0
SWIGLU_ALPHA = 1.702
MOE_BLOCK = 512
LOG2_E = 1.4426950408889634
MASK_VALUE = -1e30
SUBLANES = 8
DMA_PRIORITIES = 2

LANES = 128
HEADS_PER_LANE_GROUP = LANES // NA_HEAD_DIM
MIB = 1024 * 1024


def _params(semantics, vmem_mib):
    return pltpu.CompilerParams(dimension_semantics=semantics, vmem_limit_bytes=vmem_mib * MIB)


def _norm_mod(x, gain, shift, scale):
    ms = jnp.mean(x * x, axis=-1, keepdims=True)
    y = x * lax.rsqrt(ms + RMS_EPS) * gain
    return y * (1.0 + scale) + shift


def _nt_dot(a, b):
    return lax.dot_general(a, b, (((1,), (1,)), ((), ())), preferred_element_type=F32)


def _mod_kernel(c_ref, w_ref, b_ref, o_ref):
    c = c_ref[...]
    s = (c * jax.nn.sigmoid(c)).astype(BF16)
    o_ref[0] = jnp.dot(s, w_ref[0].astype(BF16), preferred_element_type=F32) + b_ref[0]


def _modulation(c_all, w_mod, b_mod):
    depth, d, n = w_mod.shape
    nb = c_all.shape[0]
    tn = 1024
    return pl.pallas_call(
        _mod_kernel,
        grid=(depth, n // tn),
        in_specs=[pl.BlockSpec((nb, d), lambda i, j: (0, 0)),
                  pl.BlockSpec((1, d, tn), lambda i, j: (i, 0, j)),
                  pl.BlockSpec((1, 1, tn), lambda i, j: (i, 0, j))],
        out_specs=pl.BlockSpec((1, nb, tn), lambda i, j: (i, 0, j)),
        out_shape=jax.ShapeDtypeStruct((depth, nb, n), F32),
        compiler_params=_params(("arbitrary", "arbitrary"), 40),
        name="modulation",
    )(c_all, w_mod, b_mod.reshape(depth, 1, n))


class _Layout:
    def __init__(self, bsz, s_len, l_len, with_ctx):
        self.bsz, self.s_len, self.l_len, self.with_ctx = bsz, s_len, l_len, with_ctx
        self.n_lat = bsz * s_len
        self.n_tok = self.n_lat + (bsz * l_len if with_ctx else 0)

    def tile(self, want, seq_local=False):
        tm = min(want, self.s_len)
        if self.with_ctx:
            ctx_span = self.l_len if seq_local else self.bsz * self.l_len
            tm = min(tm, ctx_span)
            assert ctx_span % tm == 0
        assert self.s_len % tm == 0
        return tm

    def mod_row(self, t, tm):
        lat = (t * tm) // self.s_len
        if not self.with_ctx:
            return lat
        return jnp.where(t < self.n_lat // tm, lat, self.bsz)


def _mod_spec(mod3, layer, which, nb, layout, tm):
    base = (layer * N_MOD + which) * nb
    d = mod3.shape[-1]
    return pl.BlockSpec((1, 1, d), lambda t, *_: (base + layout.mod_row(t, tm), 0, 0))


def _pool_kernel(xc_ref, xp_ref, xn_ref, gain_ref, sh_ref, sc_ref, gt_ref, pw_ref, ps_ref, o_ref, ext_ref,
                 *, tm, s_len, l_len, n_lat_tiles, with_ctx):
    t = pl.program_id(0)
    if with_ctx:
        is_lat = t < n_lat_tiles
        tiles_per_seq = jnp.where(is_lat, s_len // tm, l_len // tm)
        tile_in_seq = jnp.where(is_lat, t % (s_len // tm), (t - n_lat_tiles) % (l_len // tm))
        seq_len = jnp.where(is_lat, s_len, l_len)
    else:
        tiles_per_seq = s_len // tm
        tile_in_seq = t % (s_len // tm)
        seq_len = s_len
    gain, shift, scale = gain_ref[...], sh_ref[0], sc_ref[0]
    hc = _norm_mod(xc_ref[...], gain, shift, scale)
    hp = _norm_mod(xp_ref[...], gain, shift, scale)
    hn = _norm_mod(xn_ref[...], gain, shift, scale)
    ext_ref[0:POOL_HALO] = jnp.where(tile_in_seq > 0, hp, 0.0)
    ext_ref[POOL_HALO:POOL_HALO + tm] = hc
    ext_ref[POOL_HALO + tm:2 * POOL_HALO + tm] = jnp.where(tile_in_seq < tiles_per_seq - 1, hn, 0.0)
    pos = tile_in_seq * tm + lax.broadcasted_iota(I32, (tm, 1), 0)
    dg = hc.shape[1] // len(POOL_WINDOWS)
    gate, pscale = gt_ref[0], ps_ref[...]
    for g, w in enumerate(POOL_WINDOWS):
        cs = slice(g * dg, (g + 1) * dg)
        start = POOL_HALO - w // 2
        acc = ext_ref[start:start + tm, cs]
        for j in range(1, w):
            acc = acc + ext_ref[start + j:start + j + tm, cs]
        cnt = jnp.minimum(pos + w // 2, seq_len) - jnp.maximum(pos - w // 2, 0)
        diff = (acc / cnt.astype(F32) - hc[:, cs]).astype(BF16)
        y = jnp.dot(diff, pw_ref[g], preferred_element_type=F32)
        o_ref[:, cs] = xc_ref[:, cs] + gate[:, cs] * (y * pscale[:, cs])


def _pool_mixer(xt, layout, mod3, nb, layer, gain, pool_w, pool_scale):
    n_tok, d = xt.shape
    tm = layout.tile(256, seq_local=True)
    hb = tm // POOL_HALO
    n_halo_blocks = n_tok // POOL_HALO
    kern = functools.partial(_pool_kernel, tm=tm, s_len=layout.s_len, l_len=layout.l_len,
                             n_lat_tiles=layout.n_lat // tm, with_ctx=layout.with_ctx)
    return pl.pallas_call(
        kern,
        grid=(layout.n_tok // tm,),
        in_specs=[pl.BlockSpec((tm, d), lambda t: (t, 0)),
                  pl.BlockSpec((POOL_HALO, d), lambda t: (jnp.maximum(t * hb - 1, 0), 0)),
                  pl.BlockSpec((POOL_HALO, d), lambda t: (jnp.minimum((t + 1) * hb, n_halo_blocks - 1), 0)),
                  pl.BlockSpec((1, d), lambda t: (0, 0)),
                  _mod_spec(mod3, layer, 0, nb, layout, tm),
                  _mod_spec(mod3, layer, 1, nb, layout, tm),
                  _mod_spec(mod3, layer, 2, nb, layout, tm),
                  pl.BlockSpec(pool_w.shape, lambda t: (0, 0, 0)),
                  pl.BlockSpec((1, d), lambda t: (0, 0))],
        out_specs=pl.BlockSpec((tm, d), lambda t: (t, 0)),
        out_shape=jax.ShapeDtypeStruct((layout.n_tok, d), F32),
        scratch_shapes=[pltpu.VMEM((tm + 2 * POOL_HALO, d), F32)],
        compiler_params=_params(("arbitrary",), 40),
        name="pool_mixer",
    )(xt, xt, xt, gain.reshape(1, d), mod3, mod3, mod3, pool_w.astype(BF16), pool_scale.reshape(1, d))


def _router_kernel(x_ref, gain_ref, sh_ref, sc_ref, wh_ref, wl_ref, rb_ref, h_ref, idx_ref, gate_ref):
    h = _norm_mod(x_ref[...], gain_ref[...], sh_ref[0], sc_ref[0])
    h_ref[...] = h
    h_hi = h.astype(BF16)
    h_lo = (h - h_hi.astype(F32)).astype(BF16)
    wh, wl = wh_ref[...], wl_ref[...]
    logits = _nt_dot(wh, h_hi) + _nt_dot(wh, h_lo) + _nt_dot(wl, h_hi) + rb_ref[...]
    n_exp = logits.shape[0]
    e_iota = lax.broadcasted_iota(I32, logits.shape, 0)
    work = logits
    vals, idxs = [], []
    for _ in range(TOP_K):
        m = jnp.max(work, axis=0, keepdims=True)
        idx = jnp.min(jnp.where(work == m, e_iota, n_exp), axis=0, keepdims=True)
        work = jnp.where(e_iota == idx, -jnp.inf, work)
        vals.append(m)
        idxs.append(idx)
    exps = [jnp.exp(v - vals[0]) for v in vals]
    denom = exps[0] + exps[1] + exps[2] + exps[3]
    pad_rows = idx_ref.shape[0] - TOP_K
    idx_ref[...] = jnp.concatenate(idxs + [jnp.zeros((pad_rows, logits.shape[1]), I32)], axis=0)
    gate_ref[...] = jnp.concatenate([e / denom for e in exps] + [jnp.zeros((pad_rows, logits.shape[1]), F32)],
                                    axis=0)


def _router(xt, layout, mod3, nb, layer, gain, router_w, router_b):
    n_tok, d = xt.shape
    n_exp = router_w.shape[1]
    tm = layout.tile(512)
    wt = router_w.T
    w_hi = wt.astype(BF16)
    w_lo = (wt - w_hi.astype(F32)).astype(BF16)
    return pl.pallas_call(
        _router_kernel,
        grid=(layout.n_tok // tm,),
        in_specs=[pl.BlockSpec((tm, d), lambda t: (t, 0)),
                  pl.BlockSpec((1, d), lambda t: (0, 0)),
                  _mod_spec(mod3, layer, 3, nb, layout, tm),
                  _mod_spec(mod3, layer, 4, nb, layout, tm),
                  pl.BlockSpec((n_exp, d), lambda t: (0, 0)),
                  pl.BlockSpec((n_exp, d), lambda t: (0, 0)),
                  pl.BlockSpec((n_exp, 1), lambda t: (0, 0))],
        out_specs=[pl.BlockSpec((tm, d), lambda t: (t, 0)),
                   pl.BlockSpec((8, tm), lambda t: (0, t)),
                   pl.BlockSpec((8, tm), lambda t: (0, t))],
        out_shape=[jax.ShapeDtypeStruct((layout.n_tok, d), F32),
                   jax.ShapeDtypeStruct((8, layout.n_tok), I32),
                   jax.ShapeDtypeStruct((8, layout.n_tok), F32)],
        compiler_params=_params(("arbitrary",), 40),
        name="moe_router",
    )(xt, gain.reshape(1, d), mod3, mod3, w_hi, w_lo, router_b.reshape(n_exp, 1))


def _rank_kernel(idx_ref, rank_ref, cnt_ref, run_ref, *, n_exp):
    t = pl.program_id(0)

    @pl.when(t == 0)
    def _():
        run_ref[...] = jnp.zeros_like(run_ref)

    tm = idx_ref.shape[1]
    e_iota = lax.broadcasted_iota(I32, (n_exp, tm), 0)
    sel = [e_iota == idx_ref[k:k + 1, :] for k in range(TOP_K)]
    member = sel[0] | sel[1] | sel[2] | sel[3]
    member_f = jnp.where(member, 1.0, 0.0)
    strictly_before = (lax.broadcasted_iota(I32, (tm, tm), 0) < lax.broadcasted_iota(I32, (tm, tm), 1))
    prefix = jnp.dot(member_f.astype(BF16), jnp.where(strictly_before, 1.0, 0.0).astype(BF16),
                     preferred_element_type=F32)
    base = run_ref[:, 0:1] + prefix
    ranks = [jnp.sum(jnp.where(s, base, 0.0), axis=0, keepdims=True).astype(I32) for s in sel]
    pad_rows = rank_ref.shape[0] - TOP_K
    rank_ref[...] = jnp.concatenate(ranks + [jnp.zeros((pad_rows, tm), I32)], axis=0)
    run_ref[...] = run_ref[...] + jnp.sum(member_f, axis=1, keepdims=True)
    cnt_ref[...] = run_ref[...].astype(I32)


def _ranks(idx_t, n_exp):
    n_tok = idx_t.shape[1]
    tm = 512 if n_tok % 512 == 0 else 256
    return pl.pallas_call(
        functools.partial(_rank_kernel, n_exp=n_exp),
        grid=(n_tok // tm,),
        in_specs=[pl.BlockSpec((8, tm), lambda t: (0, t))],
        out_specs=[pl.BlockSpec((8, tm), lambda t: (0, t)),
                   pl.BlockSpec((n_exp, LANES), lambda t: (0, 0))],
        out_shape=[jax.ShapeDtypeStruct((8, n_tok), I32),
                   jax.ShapeDtypeStruct((n_exp, LANES), I32)],
        scratch_shapes=[pltpu.VMEM((n_exp, LANES), F32)],
        compiler_params=_params(("arbitrary",), 32),
        name="moe_rank",
    )(idx_t)


def _dest_kernel(idx_ref, rank_ref, ps_ref, dest_ref, *, n_exp):
    tm = idx_ref.shape[1]
    e_iota = lax.broadcasted_iota(I32, (n_exp, tm), 0)
    ps = ps_ref[:, 0:1]
    rows = []
    for k in range(TOP_K):
        start = jnp.sum(jnp.where(e_iota == idx_ref[k:k + 1, :], ps, 0.0), axis=0, keepdims=True)
        rows.append(start.astype(I32) + rank_ref[k:k + 1, :])
    pad_rows = dest_ref.shape[0] - TOP_K
    dest_ref[...] = jnp.concatenate(rows + [jnp.zeros((pad_rows, tm), I32)], axis=0)


def _dests(idx_t, rank_t, pad_starts):
    n_tok = idx_t.shape[1]
    n_exp = pad_starts.shape[0]
    tm = 512 if n_tok % 512 == 0 else 256
    ps = jnp.broadcast_to(pad_starts.astype(F32)[:, None], (n_exp, LANES))
    return pl.pallas_call(
        functools.partial(_dest_kernel, n_exp=n_exp),
        grid=(n_tok // tm,),
        in_specs=[pl.BlockSpec((8, tm), lambda t: (0, t)),
                  pl.BlockSpec((8, tm), lambda t: (0, t)),
                  pl.BlockSpec((n_exp, LANES), lambda t: (0, 0))],
        out_specs=pl.BlockSpec((8, tm), lambda t: (0, t)),
        out_shape=jax.ShapeDtypeStruct((8, n_tok), I32),
        compiler_params=_params(("arbitrary",), 32),
        name="moe_dest",
    )(idx_t, rank_t, ps)


def _tile_major(dest_t, tm):
    n_tok = dest_t.shape[1]
    return dest_t[:TOP_K].reshape(TOP_K, n_tok // tm, tm).transpose(1, 0, 2).reshape(-1)


def _dispatch_kernel(cnt_ref, ps_ref, h_ref, dest_ref, xs_ref, zero_ref, sem, zsem, *, n_exp, block):
    t = pl.program_id(0)
    tm = h_ref.shape[0]

    def row_copy(r, d):
        return pltpu.make_async_copy(h_ref.at[pl.ds(r, 1)], xs_ref.at[pl.ds(d, 1)], sem)

    def issue(r8, carry):
        for u in range(SUBLANES):
            r = r8 * SUBLANES + u
            for k in range(TOP_K):
                row_copy(r, dest_ref[k * tm + r]).start(priority=k % DMA_PRIORITIES)
        return carry

    lax.fori_loop(0, tm // SUBLANES, issue, 0)

    @pl.when(t == 0)
    def _():
        zero_ref[...] = jnp.zeros_like(zero_ref)

        def zero_copy(d):
            return pltpu.make_async_copy(zero_ref.at[pl.ds(0, 1)], xs_ref.at[pl.ds(d, 1)], zsem)

        for e in range(n_exp):
            cnt = cnt_ref[e]
            first = ps_ref[e] + cnt
            n_pad = (-cnt) % block

            def zissue(i, carry, first=first):
                zero_copy(first + i).start()
                return carry

            def zwait(i, carry):
                zero_copy(0).wait()
                return carry

            lax.fori_loop(0, n_pad, zissue, 0)
            lax.fori_loop(0, n_pad, zwait, 0)

        def tail_copy(j):
            return pltpu.make_async_copy(zero_ref, xs_ref.at[pl.ds(j * block, block)], zsem)

        def tail_issue(j, carry):
            tail_copy(j).start()
            return carry

        def tail_wait(j, carry):
            tail_copy(j).wait()
            return carry

        first_tail = ps_ref[n_exp] // block
        lax.fori_loop(first_tail, xs_ref.shape[0] // block, tail_issue, 0)
        lax.fori_loop(first_tail, xs_ref.shape[0] // block, tail_wait, 0)

    for _ in range(TOP_K):
        pltpu.make_async_copy(zero_ref.at[pl.ds(0, tm)], xs_ref.at[pl.ds(0, tm)], sem).wait()


def _dispatch(h, dest_flat, counts, pad_starts, n_slots, tm):
    n_tok, d = h.shape
    n_exp = counts.shape[0]
    grid_spec = pltpu.PrefetchScalarGridSpec(
        num_scalar_prefetch=2,
        grid=(n_tok // tm,),
        in_specs=[pl.BlockSpec((tm, d), lambda t, *_: (t, 0)),
                  pl.BlockSpec((TOP_K * tm,), lambda t, *_: (t,), memory_space=pltpu.SMEM)],
        out_specs=pl.BlockSpec(memory_space=pl.ANY),
        scratch_shapes=[pltpu.VMEM((MOE_BLOCK, d), F32), pltpu.SemaphoreType.DMA, pltpu.SemaphoreType.DMA],
    )
    return pl.pallas_call(
        functools.partial(_dispatch_kernel, n_exp=n_exp, block=MOE_BLOCK),
        grid_spec=grid_spec,
        out_shape=jax.ShapeDtypeStruct((n_slots, d), F32),
        compiler_params=_params(("arbitrary",), 32),
        name="moe_dispatch",
    )(counts, pad_starts, h, dest_flat)


def _expert_kernel(be_ref, nu_ref, x_ref, w1_ref, b1_ref, w2_ref, b2_ref, y_ref):
    j = pl.program_id(0)

    @pl.when(j < nu_ref[0])
    def _():
        f = w2_ref.shape[2]
        hg = jnp.dot(x_ref[...].astype(BF16), w1_ref[0, 0], preferred_element_type=F32) + b1_ref[0, 0]
        gate = jnp.minimum(hg[:, :f], SWIGLU_LIMIT)
        up = jnp.clip(hg[:, f:], -SWIGLU_LIMIT, SWIGLU_LIMIT)
        glu = gate * jax.nn.sigmoid(SWIGLU_ALPHA * gate)
        act = (glu * (up + 1.0)).astype(BF16)
        y_ref[...] = jnp.dot(act, w2_ref[0, 0], preferred_element_type=F32) + b2_ref[0, 0]

    @pl.when(j >= nu_ref[0])
    def _():
        y_ref[...] = jnp.zeros_like(y_ref)


def _experts(xs, block_expert, n_used, w1, b1, w2, b2, layer):
    n_slots, d = xs.shape
    depth, n_exp, _, f2 = w1.shape
    f = w2.shape[2]
    bm = MOE_BLOCK

    def row_block(j, be, nu):
        return (jnp.minimum(j, nu[0] - 1), 0)

    grid_spec = pltpu.PrefetchScalarGridSpec(
        num_scalar_prefetch=2,
        grid=(n_slots // bm,),
        in_specs=[pl.BlockSpec((bm, d), row_block),
                  pl.BlockSpec((1, 1, d, f2), lambda j, be, nu: (layer, be[j], 0, 0)),
                  pl.BlockSpec((1, 1, 1, f2), lambda j, be, nu: (layer, be[j], 0, 0)),
                  pl.BlockSpec((1, 1, f, d), lambda j, be, nu: (layer, be[j], 0, 0)),
                  pl.BlockSpec((1, 1, 1, d), lambda j, be, nu: (layer, be[j], 0, 0))],
        out_specs=pl.BlockSpec((bm, d), lambda j, be, nu: (j, 0)),
    )
    return pl.pallas_call(
        _expert_kernel,
        grid_spec=grid_spec,
        out_shape=jax.ShapeDtypeStruct((n_slots, d), F32),
        compiler_params=_params(("arbitrary",), 56),
        name="moe_experts",
    )(block_expert, n_used, xs, w1, b1.reshape(depth, n_exp, 1, f2), w2, b2.reshape(depth, n_exp, 1, d))


def _combine_kernel(x_ref, g_ref, dest_ref, gt_ref, fg_ref, ys_ref, o_ref, buf_ref, sem, *, final_norm):
    tm = x_ref.shape[0]

    def row_copy(k, r8, u, d):
        return pltpu.make_async_copy(ys_ref.at[pl.ds(d, 1)], buf_ref.at[k, r8, pl.ds(u, 1)], sem)

    def issue(r8, carry):
        for u in range(SUBLANES):
            for k in range(TOP_K):
                row_copy(k, r8, u, dest_ref[k * tm + r8 * SUBLANES + u]).start(priority=k % DMA_PRIORITIES)
        return carry

    lax.fori_loop(0, tm // SUBLANES, issue, 0)

    g8 = g_ref[...]
    g_cols = jnp.concatenate([g8, jnp.zeros((tm - 8, tm), F32)], axis=0).T

    for k in range(TOP_K):
        pltpu.make_async_copy(ys_ref.at[pl.ds(0, tm)], o_ref, sem).wait()

    d = x_ref.shape[1]
    acc = g_cols[:, 0:1] * buf_ref[0].reshape(tm, d)
    for k in range(1, TOP_K):
        acc = acc + g_cols[:, k:k + 1] * buf_ref[k].reshape(tm, d)
    out = x_ref[...] + gt_ref[0] * acc
    if final_norm:
        ms = jnp.mean(out * out, axis=-1, keepdims=True)
        out = out * lax.rsqrt(ms + RMS_EPS) * fg_ref[...]
    o_ref[...] = out


def _combine(xt, out_layout, gates_t, dest_flat, ys, mod3, nb, layer, final_gain, tm):
    d = xt.shape[1]
    final_norm = final_gain is not None
    fg = (final_gain if final_norm else jnp.ones((d,), F32)).reshape(1, d)
    return pl.pallas_call(
        functools.partial(_combine_kernel, final_norm=final_norm),
        grid=(out_layout.n_tok // tm,),
        in_specs=[pl.BlockSpec((tm, d), lambda t: (t, 0)),
                  pl.BlockSpec((8, tm), lambda t: (0, t)),
                  pl.BlockSpec((TOP_K * tm,), lambda t: (t,), memory_space=pltpu.SMEM),
                  _mod_spec(mod3, layer, 5, nb, out_layout, tm),
                  pl.BlockSpec((1, d), lambda t: (0, 0)),
                  pl.BlockSpec(memory_space=pl.ANY)],
        out_specs=pl.BlockSpec((tm, d), lambda t: (t, 0)),
        out_shape=jax.ShapeDtypeStruct((out_layout.n_tok, d), F32),
        scratch_shapes=[pltpu.VMEM((TOP_K, tm // SUBLANES, SUBLANES, d), F32), pltpu.SemaphoreType.DMA],
        compiler_params=_params(("arbitrary",), 32),
        name="moe_combine",
    )(xt, gates_t, dest_flat, mod3, fg, ys)


def _moe(xt, layout, out_layout, mod3, nb, layer, gain, router_w, router_b, w1, b1, w2, b2, final_gain):
    n_exp = router_w.shape[1]
    h, idx_t, gates_t = _router(xt, layout, mod3, nb, layer, gain, router_w, router_b)
    rank_t, cnt = _ranks(idx_t, n_exp)
    counts = cnt[:, 0]
    padded = (counts + MOE_BLOCK - 1) // MOE_BLOCK * MOE_BLOCK
    pad_ends = jnp.cumsum(padded)
    pad_starts = pad_ends - padded
    n_blocks = -(-layout.n_tok * TOP_K // MOE_BLOCK) + n_exp
    n_slots = n_blocks * MOE_BLOCK
    block_first_row = jnp.arange(n_blocks, dtype=I32) * MOE_BLOCK
    block_expert = jnp.minimum(jnp.sum(pad_ends[None, :] <= block_first_row[:, None], axis=1), n_exp - 1).astype(I32)
    n_used = (pad_ends[-1:] // MOE_BLOCK).astype(I32)
    dest_t = _dests(idx_t, rank_t, pad_starts)
    tm_d = layout.tile(256)
    tm_c = tm_d
    slot_bounds = jnp.concatenate([pad_starts, pad_ends[-1:]]).astype(I32)
    xs = _dispatch(h, _tile_major(dest_t, tm_d), counts, slot_bounds, n_slots, tm_d)
    ys = _experts(xs, block_expert, n_used, w1, b1, w2, b2, layer)
    return _combine(xt, out_layout, gates_t, _tile_major(dest_t, tm_c), ys, mod3, nb, layer, final_gain, tm_c)


def _nm_matmul_kernel(x_ref, gain_ref, sh_ref, sc_ref, w_ref, b_ref, o_ref, h_ref):
    @pl.when(pl.program_id(1) == 0)
    def _():
        h_ref[...] = _norm_mod(x_ref[...], gain_ref[...], sh_ref[0], sc_ref[0]).astype(BF16)

    o_ref[...] = (jnp.dot(h_ref[...], w_ref[...], preferred_element_type=F32) + b_ref[...]).astype(o_ref.dtype)


def _nm_matmul(xt, layout, mod3, nb, layer, gain, w, b, tm_want, tn):
    d = xt.shape[1]
    n = w.shape[1]
    tm = layout.tile(tm_want)
    return pl.pallas_call(
        _nm_matmul_kernel,
        grid=(layout.n_tok // tm, n // tn),
        in_specs=[pl.BlockSpec((tm, d), lambda m, j: (m, 0)),
                  pl.BlockSpec((1, d), lambda m, j: (0, 0)),
                  _mod_spec(mod3, layer, 0, nb, layout, tm),
                  _mod_spec(mod3, layer, 1, nb, layout, tm),
                  pl.BlockSpec((d, tn), lambda m, j: (0, j)),
                  pl.BlockSpec((1, tn), lambda m, j: (0, j))],
        out_specs=pl.BlockSpec((tm, tn), lambda m, j: (m, j)),
        out_shape=jax.ShapeDtypeStruct((layout.n_tok, n), BF16),
        scratch_shapes=[pltpu.VMEM((tm, d), BF16)],
        compiler_params=_params(("arbitrary", "arbitrary"), 48),
        name="norm_mod_matmul",
    )(xt, gain.reshape(1, d), mod3, mod3, w.astype(BF16), b.reshape(1, n))


def _proj_residual_kernel(a_ref, w_ref, b_ref, x_ref, gt_ref, o_ref):
    y = jnp.dot(a_ref[...], w_ref[...], preferred_element_type=F32) + b_ref[...]
    o_ref[...] = x_ref[...] + gt_ref[0] * y


def _proj_residual(a, w, b, xt, out_layout, mod3, nb, layer):
    k, n = w.shape
    tm = out_layout.tile(512)
    return pl.pallas_call(
        _proj_residual_kernel,
        grid=(out_layout.n_tok // tm,),
        in_specs=[pl.BlockSpec((tm, k), lambda m: (m, 0)),
                  pl.BlockSpec((k, n), lambda m: (0, 0)),
                  pl.BlockSpec((1, n), lambda m: (0, 0)),
                  pl.BlockSpec((tm, n), lambda m: (m, 0)),
                  _mod_spec(mod3, layer, 2, nb, out_layout, tm)],
        out_specs=pl.BlockSpec((tm, n), lambda m: (m, 0)),
        out_shape=jax.ShapeDtypeStruct((out_layout.n_tok, n), F32),
        compiler_params=_params(("arbitrary",), 48),
        name="proj_residual",
    )(a, w.astype(BF16), b.reshape(1, n), xt, mod3)


def _na_bias_slabs(rpb, rows):
    n_heads = rpb.shape[0]
    kr = min(NA_ROWS, rows)
    col = jnp.arange(GRID_W)
    c0 = jnp.clip(col - NA_COLS // 2, 0, GRID_W - NA_COLS)
    col_mask = (col[None, :] >= c0[:, None]) & (col[None, :] < c0[:, None] + NA_COLS)
    dc_idx = jnp.clip(col[None, :] - col[:, None] + NA_COLS - 1, 0, 2 * NA_COLS - 2)
    n_place = NA_ROWS
    dr = jnp.arange(n_place)[:, None] + jnp.arange(kr)[None, :]
    bias = rpb[:, dr][:, :, :, dc_idx]
    bias = jnp.where(col_mask[None, None, None], bias * LOG2_E, MASK_VALUE)
    bias = jnp.transpose(bias, (1, 0, 3, 2, 4))
    return bias.reshape(n_place, n_heads // HEADS_PER_LANE_GROUP, HEADS_PER_LANE_GROUP * GRID_W, kr * GRID_W)


def _na_kernel(q_ref, k_ref, v_ref, kc_ref, vc_ref, bias_ref, o_ref, *, rows, kr, n_groups):
    r = pl.program_id(2)
    r0 = jnp.clip(r - kr // 2, 0, rows - kr)
    base = pl.multiple_of(r0 * GRID_W, GRID_W)
    n_lat, n_ctx = kr * GRID_W, kc_ref.shape[0]
    ch = NA_KEY_CHUNK
    nq = HEADS_PER_LANE_GROUP * GRID_W
    head_of_row = lax.broadcasted_iota(I32, (nq, LANES), 0) // GRID_W
    head_of_lane = lax.broadcasted_iota(I32, (nq, LANES), 1) // NA_HEAD_DIM
    own = head_of_row == head_of_lane
    for g in range(n_groups):
        gs = slice(g * LANES, (g + 1) * LANES)
        qg = q_ref[:, gs]
        qbd = jnp.where(own, jnp.concatenate([qg] * HEADS_PER_LANE_GROUP, axis=0), jnp.zeros((), BF16))
        m = l = acc = None
        for c in range((n_lat + n_ctx) // ch):
            if c < n_lat // ch:
                rows_c = pl.ds(base + c * ch, ch)
                s = _nt_dot(qbd, k_ref[rows_c, gs]) + bias_ref[0, g, :, c * ch:(c + 1) * ch]
                vals = v_ref[rows_c, gs]
            else:
                rows_c = slice(c * ch - n_lat, (c + 1) * ch - n_lat)
                s = _nt_dot(qbd, kc_ref[rows_c, gs])
                vals = vc_ref[rows_c, gs]
            m_c = jnp.max(s, axis=-1, keepdims=True)
            if m is None:
                m = m_c
                p = jnp.exp2(s - m)
                l = jnp.sum(p, axis=-1, keepdims=True)
                acc = jnp.dot(p.astype(BF16), vals, preferred_element_type=F32)
            else:
                m_new = jnp.maximum(m, m_c)
                alpha = jnp.exp2(m - m_new)
                p = jnp.exp2(s - m_new)
                l = l * alpha + jnp.sum(p, axis=-1, keepdims=True)
                acc = acc * alpha + jnp.dot(p.astype(BF16), vals, preferred_element_type=F32)
                m = m_new
        o = jnp.where(own, acc / l, 0.0)
        og = o[0:GRID_W]
        for h in range(1, HEADS_PER_LANE_GROUP):
            og = og + o[h * GRID_W:(h + 1) * GRID_W]
        o_ref[:, gs] = og.astype(o_ref.dtype)


def _neighbourhood_attention(qkv, bias_slabs, bsz, s_len, l_len, d):
    rows = s_len // GRID_W
    kr = min(NA_ROWS, rows)
    n_split = 2
    dh = d // n_split
    n_groups = dh // LANES
    ctx_block0 = bsz * s_len // l_len

    def place(r):
        r0 = jnp.clip(r - kr // 2, 0, rows - kr)
        return r0 - r + NA_ROWS - 1

    return pl.pallas_call(
        functools.partial(_na_kernel, rows=rows, kr=kr, n_groups=n_groups),
        grid=(bsz, n_split, rows),
        in_specs=[pl.BlockSpec((GRID_W, dh), lambda b, h, r: (b * rows + r, h)),
                  pl.BlockSpec((s_len, dh), lambda b, h, r: (b, n_split + h)),
                  pl.BlockSpec((s_len, dh), lambda b, h, r: (b, 2 * n_split + h)),
                  pl.BlockSpec((l_len, dh), lambda b, h, r: (ctx_block0 + b, n_split + h)),
                  pl.BlockSpec((l_len, dh), lambda b, h, r: (ctx_block0 + b, 2 * n_split + h)),
                  pl.BlockSpec((1, n_groups, HEADS_PER_LANE_GROUP * GRID_W, kr * GRID_W),
                               lambda b, h, r: (place(r), h, 0, 0))],
        out_specs=pl.BlockSpec((GRID_W, dh), lambda b, h, r: (b * rows + r, h)),
        out_shape=jax.ShapeDtypeStruct((bsz * s_len, d), BF16),
        compiler_params=_params(("arbitrary", "arbitrary", "arbitrary"), 48),
        name="neighbourhood_attention",
    )(qkv, qkv, qkv, qkv, qkv, bias_slabs)


def _dft_cos_sin(n):
    j = jnp.arange(n, dtype=I32)
    ang = ((j[:, None] * j[None, :]) % n).astype(F32) * (2.0 * jnp.pi / n)
    norm = n ** -0.5
    return jnp.cos(ang) * norm, jnp.sin(ang) * norm


def _fnet_stage1_kernel(x_ref, gain_ref, sh_ref, sc_ref, cs_ref, o_ref):
    h = _norm_mod(x_ref[...], gain_ref[...], sh_ref[0], sc_ref[0]).astype(BF16)
    dg = cs_ref.shape[0]
    for g in range(FNET_GROUPS):
        cols = slice(g * dg, (g + 1) * dg)
        a = jnp.dot(h[:, cols], cs_ref[...], preferred_element_type=F32)
        o_ref[0, 0, :, cols] = a[:, :dg].astype(o_ref.dtype)
        o_ref[0, 1, :, cols] = a[:, dg:].astype(o_ref.dtype)


def _fnet_stage1(xt, layout, mod3, nb, layer, gain):
    d = xt.shape[1]
    dg = d // FNET_GROUPS
    tm = layout.tile(512)
    tiles_per_seq = layout.s_len // tm
    cc, sc = _dft_cos_sin(dg)
    cs = jnp.concatenate([cc, sc], axis=1).astype(BF16)
    return pl.pallas_call(
        _fnet_stage1_kernel,
        grid=(layout.n_lat // tm,),
        in_specs=[pl.BlockSpec((tm, d), lambda t: (t, 0)),
                  pl.BlockSpec((1, d), lambda t: (0, 0)),
                  _mod_spec(mod3, layer, 0, nb, layout, tm),
                  _mod_spec(mod3, layer, 1, nb, layout, tm),
                  pl.BlockSpec((dg, 2 * dg), lambda t: (0, 0))],
        out_specs=pl.BlockSpec((1, 2, tm, d), lambda t: (t // tiles_per_seq, 0, t % tiles_per_seq, 0)),
        out_shape=jax.ShapeDtypeStruct((layout.bsz, 2, layout.s_len, d), BF16),
        compiler_params=_params(("arbitrary",), 40),
        name="fnet_channel_dft",
    )(xt, gain.reshape(1, d), mod3, mod3, cs)


def _fnet_stage2_kernel(l_ref, a_ref, o_ref):
    o_ref[...] = jnp.dot(l_ref[...], a_ref[0], preferred_element_type=F32).astype(o_ref.dtype)


def _fnet_stage2(a, s_len):
    bsz, _, _, d = a.shape
    cn, sn = _dft_cos_sin(s_len)
    lhs = jnp.concatenate([cn, -sn], axis=1).astype(BF16)
    tm = min(512, s_len)
    tn = min(1024, d)
    m_tiles = s_len // tm
    return pl.pallas_call(
        _fnet_stage2_kernel,
        grid=(bsz, d // tn, m_tiles),
        in_specs=[pl.BlockSpec((tm, 2 * s_len), lambda b, j, m: (m, 0)),
                  pl.BlockSpec((1, 2 * s_len, tn), lambda b, j, m: (b, 0, j))],
        out_specs=pl.BlockSpec((tm, tn), lambda b, j, m: (b * m_tiles + m, j)),
        out_shape=jax.ShapeDtypeStruct((bsz * s_len, d), BF16),
        compiler_params=_params(("arbitrary", "arbitrary", "arbitrary"), 48),
        name="fnet_position_dft",
    )(lhs, a.reshape(bsz, 2 * s_len, d))


def kernel(x, c, ctx, c_ctx, w_mod, b_mod, norm_gain, final_gain, pool_w, pool_scale, na_w_qkv, na_b_qkv, na_rpb, na_w_o, na_b_o, fnet_w, fnet_b, router_w, router_b, exp_w1, exp_b1, exp_w2, exp_b2):
    bsz, s_len, d = x.shape
    l_len = ctx.shape[1]
    depth = w_mod.shape[0]
    last_ctx_layer = ((depth - 2) // N_MIXERS) * N_MIXERS + 1

    nb = -(-(bsz + 1) // 8) * 8
    c_all = jnp.concatenate([c, c_ctx[None], jnp.zeros((nb - bsz - 1, d), F32)], axis=0)
    mod = _modulation(c_all, w_mod, b_mod)
    mod3 = mod.reshape(depth, nb, N_MOD, d).transpose(0, 2, 1, 3).reshape(depth * N_MOD * nb, 1, d)

    w1_bf16 = exp_w1.astype(BF16)
    w2_bf16 = exp_w2.astype(BF16)
    lat_only = _Layout(bsz, s_len, l_len, False)
    with_ctx = _Layout(bsz, s_len, l_len, True)
    xt = jnp.concatenate([x.reshape(bsz * s_len, d), ctx.reshape(bsz * l_len, d)], axis=0)
    for i in range(depth):
        kind, j = i % N_MIXERS, i // N_MIXERS
        ctx_in = i <= last_ctx_layer
        ctx_out = i < last_ctx_layer
        layout_in = with_ctx if ctx_in else lat_only
        layout_out = with_ctx if ctx_out else lat_only
        if kind == 0:
            xt = _pool_mixer(xt, layout_out, mod3, nb, i, norm_gain[i, 0], pool_w[j], pool_scale[j])
        elif kind == 1:
            if ctx_in:
                k_fold = jnp.concatenate([jnp.ones((d,), F32), jnp.full((d,), NA_HEAD_DIM ** -0.5 * LOG2_E, F32),
                                          jnp.ones((d,), F32)])
                qkv = _nm_matmul(xt, layout_in, mod3, nb, i, norm_gain[i, 0], na_w_qkv[j] * k_fold,
                                 na_b_qkv[j] * k_fold, 1024, min(1024, d))
                slabs = _na_bias_slabs(na_rpb[j], s_len // GRID_W)
                att = _neighbourhood_attention(qkv, slabs, bsz, s_len, l_len, d)
                xt = _proj_residual(att, na_w_o[j], na_b_o[j], xt, layout_out, mod3, nb, i)
            else:
                raise NotImplementedError("neighbourhood attention without a live context stream")
        else:
            a = _fnet_stage1(xt, layout_out, mod3, nb, i, norm_gain[i, 0])
            f = _fnet_stage2(a, s_len)
            xt = _proj_residual(f, fnet_w[j], fnet_b[j], xt, layout_out, mod3, nb, i)
        xt = _moe(xt, layout_out, layout_out, mod3, nb, i, norm_gain[i, 1], router_w[i], router_b[i],
                  w1_bf16, exp_b1, w2_bf16, exp_b2, final_gain if i == depth - 1 else None)
    return xt[:bsz * s_len].reshape(bsz, s_len, d)
```

```python
import functools

import jax
import jax.numpy as jnp
from jax import lax
from jax.experimental import pallas as pl
from jax.experimental.pallas import tpu as pltpu

F32 = jnp.float32
BF16 = jnp.bfloat16
I32 = jnp.int32

GRID_W = 64
N_MIXERS = 3
N_MOD = 6
RMS_EPS = 1e-6
POOL_WINDOWS = (2, 4, 8, 16)
POOL_HALO = 8
NA_HEAD_DIM = 32
NA_ROWS = 8
NA_COLS = 16
NA_KEY_CHUNK = 256
FNET_GROUPS = 4
TOP_K = 4
SWIGLU_LIMIT = 7.0
SWIGLU_ALPHA = 1.702
MOE_BLOCK = 512
LOG2_E = 1.4426950408889634
MASK_VALUE = -1e30
SUBLANES = 8
DMA_PRIORITIES = 2

LANES = 128
HEADS_PER_LANE_GROUP = LANES // NA_HEAD_DIM
MIB = 1024 * 1024


def _params(semantics, vmem_mib):
    return pltpu.CompilerParams(dimension_semantics=semantics, vmem_limit_bytes=vmem_mib * MIB)


def _norm_mod(x, gain, shift, scale):
    ms = jnp.mean(x * x, axis=-1, keepdims=True)
    y = x * lax.rsqrt(ms + RMS_EPS) * gain
    return y * (1.0 + scale) + shift


def _nt_dot(a, b):
    return lax.dot_general(a, b, (((1,), (1,)), ((), ())), preferred_element_type=F32)


def _mod_kernel(c_ref, w_ref, b_ref, o_ref):
    c = c_ref[...]
    s = (c * jax.nn.sigmoid(c)).astype(BF16)
    o_ref[0] = jnp.dot(s, w_ref[0].astype(BF16), preferred_element_type=F32) + b_ref[0]


def _modulation(c_all, w_mod, b_mod):
    depth, d, n = w_mod.shape
    nb = c_all.shape[0]
    tn = 1024
    return pl.pallas_call(
        _mod_kernel,
        grid=(depth, n // tn),
        in_specs=[pl.BlockSpec((nb, d), lambda i, j: (0, 0)),
                  pl.BlockSpec((1, d, tn), lambda i, j: (i, 0, j)),
                  pl.BlockSpec((1, 1, tn), lambda i, j: (i, 0, j))],
        out_specs=pl.BlockSpec((1, nb, tn), lambda i, j: (i, 0, j)),
        out_shape=jax.ShapeDtypeStruct((depth, nb, n), F32),
        compiler_params=_params(("arbitrary", "arbitrary"), 40),
        name="modulation",
    )(c_all, w_mod, b_mod.reshape(depth, 1, n))


class _Layout:
    def __init__(self, bsz, s_len, l_len, with_ctx):
        self.bsz, self.s_len, self.l_len, self.with_ctx = bsz, s_len, l_len, with_ctx
        self.n_lat = bsz * s_len
        self.n_tok = self.n_lat + (bsz * l_len if with_ctx else 0)

    def tile(self, want, seq_local=False):
        tm = min(want, self.s_len)
        if self.with_ctx:
            ctx_span = self.l_len if seq_local else self.bsz * self.l_len
            tm = min(tm, ctx_span)
            assert ctx_span % tm == 0
        assert self.s_len % tm == 0
        return tm

    def mod_row(self, t, tm):
        lat = (t * tm) // self.s_len
        if not self.with_ctx:
            return lat
        return jnp.where(t < self.n_lat // tm, lat, self.bsz)


def _mod_spec(mod3, layer, which, nb, layout, tm):
    base = (layer * N_MOD + which) * nb
    d = mod3.shape[-1]
    return pl.BlockSpec((1, 1, d), lambda t, *_: (base + layout.mod_row(t, tm), 0, 0))


def _pool_kernel(*refs, tm, s_len, l_len, n_lat_tiles, with_ctx, split_src):
    n_src = 6 if split_src else 3
    src_refs, (gain_ref, sh_ref, sc_ref, gt_ref, pw_ref, ps_ref, o_ref, ext_ref) = refs[:n_src], refs[n_src:]
    t = pl.program_id(0)
    if with_ctx:
        is_lat = t < n_lat_tiles
        tiles_per_seq = jnp.where(is_lat, s_len // tm, l_len // tm)
        tile_in_seq = jnp.where(is_lat, t % (s_len // tm), (t - n_lat_tiles) % (l_len // tm))
        seq_len = jnp.where(is_lat, s_len, l_len)
    else:
        tiles_per_seq = s_len // tm
        tile_in_seq = t % (s_len // tm)
        seq_len = s_len
    if split_src:
        xc, xp, xn = (jnp.where(is_lat, a[...], b[...]) for a, b in zip(src_refs[:3], src_refs[3:]))
    else:
        xc, xp, xn = (a[...] for a in src_refs)
    gain, shift, scale = gain_ref[...], sh_ref[0], sc_ref[0]
    hc = _norm_mod(xc, gain, shift, scale)
    hp = _norm_mod(xp, gain, shift, scale)
    hn = _norm_mod(xn, gain, shift, scale)
    ext_ref[0:POOL_HALO] = jnp.where(tile_in_seq > 0, hp, 0.0)
    ext_ref[POOL_HALO:POOL_HALO + tm] = hc
    ext_ref[POOL_HALO + tm:2 * POOL_HALO + tm] = jnp.where(tile_in_seq < tiles_per_seq - 1, hn, 0.0)
    pos = tile_in_seq * tm + lax.broadcasted_iota(I32, (tm, 1), 0)
    dg = hc.shape[1] // len(POOL_WINDOWS)
    gate, pscale = gt_ref[0], ps_ref[...]
    for g, w in enumerate(POOL_WINDOWS):
        cs = slice(g * dg, (g + 1) * dg)
        start = POOL_HALO - w // 2
        acc = ext_ref[start:start + tm, cs]
        for j in range(1, w):
            acc = acc + ext_ref[start + j:start + j + tm, cs]
        cnt = jnp.minimum(pos + w // 2, seq_len) - jnp.maximum(pos - w // 2, 0)
        diff = (acc / cnt.astype(F32) - hc[:, cs]).astype(BF16)
        y = jnp.dot(diff, pw_ref[g], preferred_element_type=F32)
        o_ref[:, cs] = xc[:, cs] + gate[:, cs] * (y * pscale[:, cs])


def _pool_mixer(x_lat, x_ctx, layout, mod3, nb, layer, gain, pool_w, pool_scale):
    d = x_lat.shape[1]
    tm = layout.tile(256, seq_local=True)
    hb = tm // POOL_HALO
    n_lat_tiles = layout.n_lat // tm
    split_src = x_ctx is not None

    def src_specs(n_rows, first_tile):
        def local(t):
            return jnp.clip(t - first_tile, 0, n_rows // tm - 1)
        return [pl.BlockSpec((tm, d), lambda t: (local(t), 0)),
                pl.BlockSpec((POOL_HALO, d), lambda t: (jnp.maximum(local(t) * hb - 1, 0), 0)),
                pl.BlockSpec((POOL_HALO, d), lambda t: (jnp.minimum((local(t) + 1) * hb, n_rows // POOL_HALO - 1), 0))]

    specs = src_specs(x_lat.shape[0], 0)
    srcs = [x_lat] * 3
    if split_src:
        specs += src_specs(x_ctx.shape[0], n_lat_tiles)
        srcs += [x_ctx] * 3
    kern = functools.partial(_pool_kernel, tm=tm, s_len=layout.s_len, l_len=layout.l_len,
                             n_lat_tiles=n_lat_tiles, with_ctx=layout.with_ctx, split_src=split_src)
    return pl.pallas_call(
        kern,
        grid=(layout.n_tok // tm,),
        in_specs=specs + [
                  pl.BlockSpec((1, d), lambda t: (0, 0)),
                  _mod_spec(mod3, layer, 0, nb, layout, tm),
                  _mod_spec(mod3, layer, 1, nb, layout, tm),
                  _mod_spec(mod3, layer, 2, nb, layout, tm),
                  pl.BlockSpec(pool_w.shape, lambda t: (0, 0, 0)),
                  pl.BlockSpec((1, d), lambda t: (0, 0))],
        out_specs=pl.BlockSpec((tm, d), lambda t: (t, 0)),
        out_shape=jax.ShapeDtypeStruct((layout.n_tok, d), F32),
        scratch_shapes=[pltpu.VMEM((tm + 2 * POOL_HALO, d), F32)],
        compiler_params=_params(("arbitrary",), 40),
        name="pool_mixer",
    )(*srcs, gain.reshape(1, d), mod3, mod3, mod3, pool_w.astype(BF16), pool_scale.reshape(1, d))


def _router_kernel(x_ref, gain_ref, sh_ref, sc_ref, wh_ref, wl_ref, rb_ref, h_ref, idx_ref, gate_ref):
    h = _norm_mod(x_ref[...], gain_ref[...], sh_ref[0], sc_ref[0])
    h_ref[...] = h
    h_hi = h.astype(BF16)
    h_lo = (h - h_hi.astype(F32)).astype(BF16)
    wh, wl = wh_ref[...], wl_ref[...]
    logits = _nt_dot(wh, h_hi) + _nt_dot(wh, h_lo) + _nt_dot(wl, h_hi) + rb_ref[...]
    n_exp = logits.shape[0]
    e_iota = lax.broadcasted_iota(I32, logits.shape, 0)
    work = logits
    vals, idxs = [], []
    for _ in range(TOP_K):
        m = jnp.max(work, axis=0, keepdims=True)
        idx = jnp.min(jnp.where(work == m, e_iota, n_exp), axis=0, keepdims=True)
        work = jnp.where(e_iota == idx, -jnp.inf, work)
        vals.append(m)
        idxs.append(idx)
    exps = [jnp.exp(v - vals[0]) for v in vals]
    denom = exps[0] + exps[1] + exps[2] + exps[3]
    pad_rows = idx_ref.shape[0] - TOP_K
    idx_ref[...] = jnp.concatenate(idxs + [jnp.zeros((pad_rows, logits.shape[1]), I32)], axis=0)
    gate_ref[...] = jnp.concatenate([e / denom for e in exps] + [jnp.zeros((pad_rows, logits.shape[1]), F32)],
                                    axis=0)


def _router(xt, layout, mod3, nb, layer, gain, router_w, router_b):
    n_tok, d = xt.shape
    n_exp = router_w.shape[1]
    tm = layout.tile(512)
    wt = router_w.T
    w_hi = wt.astype(BF16)
    w_lo = (wt - w_hi.astype(F32)).astype(BF16)
    return pl.pallas_call(
        _router_kernel,
        grid=(layout.n_tok // tm,),
        in_specs=[pl.BlockSpec((tm, d), lambda t: (t, 0)),
                  pl.BlockSpec((1, d), lambda t: (0, 0)),
                  _mod_spec(mod3, layer, 3, nb, layout, tm),
                  _mod_spec(mod3, layer, 4, nb, layout, tm),
                  pl.BlockSpec((n_exp, d), lambda t: (0, 0)),
                  pl.BlockSpec((n_exp, d), lambda t: (0, 0)),
                  pl.BlockSpec((n_exp, 1), lambda t: (0, 0))],
        out_specs=[pl.BlockSpec((tm, d), lambda t: (t, 0)),
                   pl.BlockSpec((8, tm), lambda t: (0, t)),
                   pl.BlockSpec((8, tm), lambda t: (0, t))],
        out_shape=[jax.ShapeDtypeStruct((layout.n_tok, d), F32),
                   jax.ShapeDtypeStruct((8, layout.n_tok), I32),
                   jax.ShapeDtypeStruct((8, layout.n_tok), F32)],
        compiler_params=_params(("arbitrary",), 40),
        name="moe_router",
    )(xt, gain.reshape(1, d), mod3, mod3, w_hi, w_lo, router_b.reshape(n_exp, 1))


def _rank_kernel(idx_ref, rank_ref, cnt_ref, run_ref, *, n_exp):
    t = pl.program_id(0)

    @pl.when(t == 0)
    def _():
        run_ref[...] = jnp.zeros_like(run_ref)

    tm = idx_ref.shape[1]
    e_iota = lax.broadcasted_iota(I32, (n_exp, tm), 0)
    sel = [e_iota == idx_ref[k:k + 1, :] for k in range(TOP_K)]
    member = sel[0] | sel[1] | sel[2] | sel[3]
    member_f = jnp.where(member, 1.0, 0.0)
    strictly_before = (lax.broadcasted_iota(I32, (tm, tm), 0) < lax.broadcasted_iota(I32, (tm, tm), 1))
    prefix = jnp.dot(member_f.astype(BF16), jnp.where(strictly_before, 1.0, 0.0).astype(BF16),
                     preferred_element_type=F32)
    base = run_ref[:, 0:1] + prefix
    ranks = [jnp.sum(jnp.where(s, base, 0.0), axis=0, keepdims=True).astype(I32) for s in sel]
    pad_rows = rank_ref.shape[0] - TOP_K
    rank_ref[...] = jnp.concatenate(ranks + [jnp.zeros((pad_rows, tm), I32)], axis=0)
    run_ref[...] = run_ref[...] + jnp.sum(member_f, axis=1, keepdims=True)
    cnt_ref[...] = run_ref[...].astype(I32)


def _ranks(idx_t, n_exp):
    n_tok = idx_t.shape[1]
    tm = 512 if n_tok % 512 == 0 else 256
    return pl.pallas_call(
        functools.partial(_rank_kernel, n_exp=n_exp),
        grid=(n_tok // tm,),
        in_specs=[pl.BlockSpec((8, tm), lambda t: (0, t))],
        out_specs=[pl.BlockSpec((8, tm), lambda t: (0, t)),
                   pl.BlockSpec((n_exp, LANES), lambda t: (0, 0))],
        out_shape=[jax.ShapeDtypeStruct((8, n_tok), I32),
                   jax.ShapeDtypeStruct((n_exp, LANES), I32)],
        scratch_shapes=[pltpu.VMEM((n_exp, LANES), F32)],
        compiler_params=_params(("arbitrary",), 32),
        name="moe_rank",
    )(idx_t)


def _dest_kernel(idx_ref, rank_ref, ps_ref, dest_ref, *, n_exp):
    tm = idx_ref.shape[1]
    e_iota = lax.broadcasted_iota(I32, (n_exp, tm), 0)
    ps = ps_ref[:, 0:1]
    rows = []
    for k in range(TOP_K):
        start = jnp.sum(jnp.where(e_iota == idx_ref[k:k + 1, :], ps, 0.0), axis=0, keepdims=True)
        rows.append(start.astype(I32) + rank_ref[k:k + 1, :])
    pad_rows = dest_ref.shape[0] - TOP_K
    dest_ref[...] = jnp.concatenate(rows + [jnp.zeros((pad_rows, tm), I32)], axis=0)


def _dests(idx_t, rank_t, pad_starts):
    n_tok = idx_t.shape[1]
    n_exp = pad_starts.shape[0]
    tm = 512 if n_tok % 512 == 0 else 256
    ps = jnp.broadcast_to(pad_starts.astype(F32)[:, None], (n_exp, LANES))
    return pl.pallas_call(
        functools.partial(_dest_kernel, n_exp=n_exp),
        grid=(n_tok // tm,),
        in_specs=[pl.BlockSpec((8, tm), lambda t: (0, t)),
                  pl.BlockSpec((8, tm), lambda t: (0, t)),
                  pl.BlockSpec((n_exp, LANES), lambda t: (0, 0))],
        out_specs=pl.BlockSpec((8, tm), lambda t: (0, t)),
        out_shape=jax.ShapeDtypeStruct((8, n_tok), I32),
        compiler_params=_params(("arbitrary",), 32),
        name="moe_dest",
    )(idx_t, rank_t, ps)


def _tile_major(dest_t, tm):
    n_tok = dest_t.shape[1]
    return dest_t[:TOP_K].reshape(TOP_K, n_tok // tm, tm).transpose(1, 0, 2).reshape(-1)


def _dispatch_kernel(cnt_ref, ps_ref, h_ref, dest_ref, xs_ref, zero_ref, sem, zsem, *, n_exp, block):
    t = pl.program_id(0)
    tm = h_ref.shape[0]

    def row_copy(r, d):
        return pltpu.make_async_copy(h_ref.at[pl.ds(r, 1)], xs_ref.at[pl.ds(d, 1)], sem)

    def issue(r8, carry):
        for u in range(SUBLANES):
            r = r8 * SUBLANES + u
            for k in range(TOP_K):
                row_copy(r, dest_ref[k * tm + r]).start(priority=k % DMA_PRIORITIES)
        return carry

    lax.fori_loop(0, tm // SUBLANES, issue, 0)

    @pl.when(t == 0)
    def _():
        zero_ref[...] = jnp.zeros_like(zero_ref)

        def zero_copy(d):
            return pltpu.make_async_copy(zero_ref.at[pl.ds(0, 1)], xs_ref.at[pl.ds(d, 1)], zsem)

        for e in range(n_exp):
            cnt = cnt_ref[e]
            first = ps_ref[e] + cnt
            n_pad = (-cnt) % block

            def zissue(i, carry, first=first):
                zero_copy(first + i).start()
                return carry

            def zwait(i, carry):
                zero_copy(0).wait()
                return carry

            lax.fori_loop(0, n_pad, zissue, 0)
            lax.fori_loop(0, n_pad, zwait, 0)

        def tail_copy(j):
            return pltpu.make_async_copy(zero_ref, xs_ref.at[pl.ds(j * block, block)], zsem)

        def tail_issue(j, carry):
            tail_copy(j).start()
            return carry

        def tail_wait(j, carry):
            tail_copy(j).wait()
            return carry

        first_tail = ps_ref[n_exp] // block
        lax.fori_loop(first_tail, xs_ref.shape[0] // block, tail_issue, 0)
        lax.fori_loop(first_tail, xs_ref.shape[0] // block, tail_wait, 0)

    for _ in range(TOP_K):
        pltpu.make_async_copy(zero_ref.at[pl.ds(0, tm)], xs_ref.at[pl.ds(0, tm)], sem).wait()


def _dispatch(h, dest_flat, counts, pad_starts, n_slots, tm):
    n_tok, d = h.shape
    n_exp = counts.shape[0]
    grid_spec = pltpu.PrefetchScalarGridSpec(
        num_scalar_prefetch=2,
        grid=(n_tok // tm,),
        in_specs=[pl.BlockSpec((tm, d), lambda t, *_: (t, 0)),
                  pl.BlockSpec((TOP_K * tm,), lambda t, *_: (t,), memory_space=pltpu.SMEM)],
        out_specs=pl.BlockSpec(memory_space=pl.ANY),
        scratch_shapes=[pltpu.VMEM((MOE_BLOCK, d), F32), pltpu.SemaphoreType.DMA, pltpu.SemaphoreType.DMA],
    )
    return pl.pallas_call(
        functools.partial(_dispatch_kernel, n_exp=n_exp, block=MOE_BLOCK),
        grid_spec=grid_spec,
        out_shape=jax.ShapeDtypeStruct((n_slots, d), F32),
        compiler_params=_params(("arbitrary",), 32),
        name="moe_dispatch",
    )(counts, pad_starts, h, dest_flat)


def _expert_kernel(be_ref, nu_ref, nv_ref, x_ref, w1_ref, b1_ref, w2_ref, b2_ref, y_ref):
    n_valid = nv_ref[pl.program_id(0)]
    half = x_ref.shape[0] // 2

    def ffn(x):
        f = w2_ref.shape[2]
        hg = jnp.dot(x.astype(BF16), w1_ref[0, 0].astype(BF16), preferred_element_type=F32) + b1_ref[0, 0]
        gate = jnp.minimum(hg[:, :f], SWIGLU_LIMIT)
        up = jnp.clip(hg[:, f:], -SWIGLU_LIMIT, SWIGLU_LIMIT)
        glu = gate * jax.nn.sigmoid(SWIGLU_ALPHA * gate)
        act = (glu * (up + 1.0)).astype(BF16)
        return jnp.dot(act, w2_ref[0, 0], preferred_element_type=F32) + b2_ref[0, 0]

    @pl.when(n_valid > half)
    def _():
        y_ref[...] = ffn(x_ref[...])

    @pl.when((n_valid > 0) & (n_valid <= half))
    def _():
        y_ref[0:half] = ffn(x_ref[0:half])
        y_ref[half:] = jnp.zeros((x_ref.shape[0] - half, y_ref.shape[1]), y_ref.dtype)

    @pl.when(n_valid == 0)
    def _():
        y_ref[...] = jnp.zeros_like(y_ref)


def _experts(xs, block_expert, n_used, n_valid, w1, b1, w2, b2, layer):
    n_slots, d = xs.shape
    depth, n_exp, _, f2 = w1.shape
    f = w2.shape[2]
    bm = MOE_BLOCK

    def row_block(j, be, nu, nv):
        return (jnp.minimum(j, nu[0] - 1), 0)

    def expert_block(j, be, nu, nv):
        return (layer, be[j], 0, 0)

    grid_spec = pltpu.PrefetchScalarGridSpec(
        num_scalar_prefetch=3,
        grid=(n_slots // bm,),
        in_specs=[pl.BlockSpec((bm, d), row_block),
                  pl.BlockSpec((1, 1, d, f2), expert_block),
                  pl.BlockSpec((1, 1, 1, f2), expert_block),
                  pl.BlockSpec((1, 1, f, d), expert_block),
                  pl.BlockSpec((1, 1, 1, d), expert_block)],
        out_specs=pl.BlockSpec((bm, d), lambda j, be, nu, nv: (j, 0)),
    )
    return pl.pallas_call(
        _expert_kernel,
        grid_spec=grid_spec,
        out_shape=jax.ShapeDtypeStruct((n_slots, d), F32),
        compiler_params=_params(("arbitrary",), 56),
        name="moe_experts",
    )(block_expert, n_used, n_valid, xs, w1, b1.reshape(depth, n_exp, 1, f2), w2, b2.reshape(depth, n_exp, 1, d))


def _combine_kernel(x_ref, g_ref, dest_ref, gt_ref, fg_ref, ys_ref, o_ref, buf_ref, sem, *, final_norm):
    tm = x_ref.shape[0]

    def row_copy(k, r8, u, d):
        return pltpu.make_async_copy(ys_ref.at[pl.ds(d, 1)], buf_ref.at[k, r8, pl.ds(u, 1)], sem)

    def issue(r8, carry):
        for u in range(SUBLANES):
            for k in range(TOP_K):
                row_copy(k, r8, u, dest_ref[k * tm + r8 * SUBLANES + u]).start(priority=k % DMA_PRIORITIES)
        return carry

    lax.fori_loop(0, tm // SUBLANES, issue, 0)

    g8 = g_ref[...]
    g_cols = jnp.concatenate([g8, jnp.zeros((tm - 8, tm), F32)], axis=0).T

    for k in range(TOP_K):
        pltpu.make_async_copy(ys_ref.at[pl.ds(0, tm)], o_ref, sem).wait()

    d = x_ref.shape[1]
    acc = g_cols[:, 0:1] * buf_ref[0].reshape(tm, d)
    for k in range(1, TOP_K):
        acc = acc + g_cols[:, k:k + 1] * buf_ref[k].reshape(tm, d)
    out = x_ref[...] + gt_ref[0] * acc
    if final_norm:
        ms = jnp.mean(out * out, axis=-1, keepdims=True)
        out = out * lax.rsqrt(ms + RMS_EPS) * fg_ref[...]
    o_ref[...] = out


def _combine(xt, out_layout, gates_t, dest_flat, ys, mod3, nb, layer, final_gain, tm):
    d = xt.shape[1]
    final_norm = final_gain is not None
    fg = (final_gain if final_norm else jnp.ones((d,), F32)).reshape(1, d)
    return pl.pallas_call(
        functools.partial(_combine_kernel, final_norm=final_norm),
        grid=(out_layout.n_tok // tm,),
        in_specs=[pl.BlockSpec((tm, d), lambda t: (t, 0)),
                  pl.BlockSpec((8, tm), lambda t: (0, t)),
                  pl.BlockSpec((TOP_K * tm,), lambda t: (t,), memory_space=pltpu.SMEM),
                  _mod_spec(mod3, layer, 5, nb, out_layout, tm),
                  pl.BlockSpec((1, d), lambda t: (0, 0)),
                  pl.BlockSpec(memory_space=pl.ANY)],
        out_specs=pl.BlockSpec((tm, d), lambda t: (t, 0)),
        out_shape=jax.ShapeDtypeStruct((out_layout.n_tok, d), F32),
        scratch_shapes=[pltpu.VMEM((TOP_K, tm // SUBLANES, SUBLANES, d), F32), pltpu.SemaphoreType.DMA],
        compiler_params=_params(("arbitrary",), 32),
        name="moe_combine",
    )(xt, gates_t, dest_flat, mod3, fg, ys)


def _moe(xt, layout, out_layout, mod3, nb, layer, gain, router_w, router_b, w1, b1, w2, b2, final_gain):
    n_exp = router_w.shape[1]
    h, idx_t, gates_t = _router(xt, layout, mod3, nb, layer, gain, router_w, router_b)
    rank_t, cnt = _ranks(idx_t, n_exp)
    counts = cnt[:, 0]
    padded = (counts + MOE_BLOCK - 1) // MOE_BLOCK * MOE_BLOCK
    pad_ends = jnp.cumsum(padded)
    pad_starts = pad_ends - padded
    n_blocks = -(-layout.n_tok * TOP_K // MOE_BLOCK) + n_exp
    n_slots = n_blocks * MOE_BLOCK
    block_first_row = jnp.arange(n_blocks, dtype=I32) * MOE_BLOCK
    block_expert = jnp.minimum(jnp.sum(pad_ends[None, :] <= block_first_row[:, None], axis=1), n_exp - 1).astype(I32)
    n_used = (pad_ends[-1:] // MOE_BLOCK).astype(I32)
    dest_t = _dests(idx_t, rank_t, pad_starts)
    tm_d = layout.tile(256)
    tm_c = tm_d
    slot_bounds = jnp.concatenate([pad_starts, pad_ends[-1:]]).astype(I32)
    xs = _dispatch(h, _tile_major(dest_t, tm_d), counts, slot_bounds, n_slots, tm_d)
    rows_before = block_first_row - pad_starts[block_expert]
    n_valid = jnp.clip(counts[block_expert] - rows_before, 0, MOE_BLOCK).astype(I32)
    ys = _experts(xs, block_expert, n_used, n_valid, w1, b1, w2, b2, layer)
    return _combine(xt, out_layout, gates_t, _tile_major(dest_t, tm_c), ys, mod3, nb, layer, final_gain, tm_c)


def _nm_matmul_kernel(x_ref, gain_ref, sh_ref, sc_ref, w_ref, b_ref, o_ref, h_ref):
    @pl.when(pl.program_id(1) == 0)
    def _():
        h_ref[...] = _norm_mod(x_ref[...], gain_ref[...], sh_ref[0], sc_ref[0]).astype(BF16)

    o_ref[...] = (jnp.dot(h_ref[...], w_ref[...], preferred_element_type=F32) + b_ref[...]).astype(o_ref.dtype)


def _nm_matmul(xt, layout, mod3, nb, layer, gain, w, b, tm_want, tn):
    d = xt.shape[1]
    n = w.shape[1]
    tm = layout.tile(tm_want)
    return pl.pallas_call(
        _nm_matmul_kernel,
        grid=(layout.n_tok // tm, n // tn),
        in_specs=[pl.BlockSpec((tm, d), lambda m, j: (m, 0)),
                  pl.BlockSpec((1, d), lambda m, j: (0, 0)),
                  _mod_spec(mod3, layer, 0, nb, layout, tm),
                  _mod_spec(mod3, layer, 1, nb, layout, tm),
                  pl.BlockSpec((d, tn), lambda m, j: (0, j)),
                  pl.BlockSpec((1, tn), lambda m, j: (0, j))],
        out_specs=pl.BlockSpec((tm, tn), lambda m, j: (m, j)),
        out_shape=jax.ShapeDtypeStruct((layout.n_tok, n), BF16),
        scratch_shapes=[pltpu.VMEM((tm, d), BF16)],
        compiler_params=_params(("arbitrary", "arbitrary"), 48),
        name="norm_mod_matmul",
    )(xt, gain.reshape(1, d), mod3, mod3, w.astype(BF16), b.reshape(1, n))


def _proj_residual_kernel(a_ref, w_ref, b_ref, x_ref, gt_ref, o_ref):
    y = jnp.dot(a_ref[...], w_ref[...], preferred_element_type=F32) + b_ref[...]
    o_ref[...] = x_ref[...] + gt_ref[0] * y


def _proj_residual(a, w, b, xt, out_layout, mod3, nb, layer):
    k, n = w.shape
    tm = out_layout.tile(512)
    return pl.pallas_call(
        _proj_residual_kernel,
        grid=(out_layout.n_tok // tm,),
        in_specs=[pl.BlockSpec((tm, k), lambda m: (m, 0)),
                  pl.BlockSpec((k, n), lambda m: (0, 0)),
                  pl.BlockSpec((1, n), lambda m: (0, 0)),
                  pl.BlockSpec((tm, n), lambda m: (m, 0)),
                  _mod_spec(mod3, layer, 2, nb, out_layout, tm)],
        out_specs=pl.BlockSpec((tm, n), lambda m: (m, 0)),
        out_shape=jax.ShapeDtypeStruct((out_layout.n_tok, n), F32),
        compiler_params=_params(("arbitrary",), 48),
        name="proj_residual",
    )(a, w.astype(BF16), b.reshape(1, n), xt, mod3)


def _na_bias_slabs(rpb, rows):
    n_heads = rpb.shape[0]
    kr = min(NA_ROWS, rows)
    col = jnp.arange(GRID_W)
    c0 = jnp.clip(col - NA_COLS // 2, 0, GRID_W - NA_COLS)
    col_mask = (col[None, :] >= c0[:, None]) & (col[None, :] < c0[:, None] + NA_COLS)
    dc_idx = jnp.clip(col[None, :] - col[:, None] + NA_COLS - 1, 0, 2 * NA_COLS - 2)
    n_place = NA_ROWS
    dr = jnp.arange(n_place)[:, None] + jnp.arange(kr)[None, :]
    bias = rpb[:, dr][:, :, :, dc_idx]
    bias = jnp.where(col_mask[None, None, None], bias * LOG2_E, MASK_VALUE)
    bias = jnp.transpose(bias, (1, 0, 3, 2, 4))
    return bias.reshape(n_place, n_heads // HEADS_PER_LANE_GROUP, HEADS_PER_LANE_GROUP * GRID_W, kr * GRID_W)


def _na_kernel(q_ref, k_ref, v_ref, kc_ref, vc_ref, bias_ref, o_ref, *, rows, kr, n_groups):
    r = pl.program_id(2)
    r0 = jnp.clip(r - kr // 2, 0, rows - kr)
    base = pl.multiple_of(r0 * GRID_W, GRID_W)
    n_lat, n_ctx = kr * GRID_W, kc_ref.shape[0]
    ch = NA_KEY_CHUNK
    nq = HEADS_PER_LANE_GROUP * GRID_W
    head_of_row = lax.broadcasted_iota(I32, (nq, LANES), 0) // GRID_W
    head_of_lane = lax.broadcasted_iota(I32, (nq, LANES), 1) // NA_HEAD_DIM
    own = head_of_row == head_of_lane
    for g in range(n_groups):
        gs = slice(g * LANES, (g + 1) * LANES)
        qg = q_ref[:, gs]
        qbd = jnp.where(own, jnp.concatenate([qg] * HEADS_PER_LANE_GROUP, axis=0), jnp.zeros((), BF16))
        m = l = acc = None
        for c in range((n_lat + n_ctx) // ch):
            if c < n_lat // ch:
                rows_c = pl.ds(base + c * ch, ch)
                s = _nt_dot(qbd, k_ref[rows_c, gs]) + bias_ref[0, g, :, c * ch:(c + 1) * ch]
                vals = v_ref[rows_c, gs]
            else:
                rows_c = slice(c * ch - n_lat, (c + 1) * ch - n_lat)
                s = _nt_dot(qbd, kc_ref[rows_c, gs])
                vals = vc_ref[rows_c, gs]
            m_c = jnp.max(s, axis=-1, keepdims=True)
            if m is None:
                m = m_c
                p = jnp.exp2(s - m)
                l = jnp.sum(p, axis=-1, keepdims=True)
                acc = jnp.dot(p.astype(BF16), vals, preferred_element_type=F32)
            else:
                m_new = jnp.maximum(m, m_c)
                alpha = jnp.exp2(m - m_new)
                p = jnp.exp2(s - m_new)
                l = l * alpha + jnp.sum(p, axis=-1, keepdims=True)
                acc = acc * alpha + jnp.dot(p.astype(BF16), vals, preferred_element_type=F32)
                m = m_new
        o = jnp.where(own, acc / l, 0.0)
        og = o[0:GRID_W]
        for h in range(1, HEADS_PER_LANE_GROUP):
            og = og + o[h * GRID_W:(h + 1) * GRID_W]
        o_ref[:, gs] = og.astype(o_ref.dtype)


def _neighbourhood_attention(qkv, bias_slabs, bsz, s_len, l_len, d):
    rows = s_len // GRID_W
    kr = min(NA_ROWS, rows)
    n_split = 2
    dh = d // n_split
    n_groups = dh // LANES
    ctx_block0 = bsz * s_len // l_len

    def place(r):
        r0 = jnp.clip(r - kr // 2, 0, rows - kr)
        return r0 - r + NA_ROWS - 1

    return pl.pallas_call(
        functools.partial(_na_kernel, rows=rows, kr=kr, n_groups=n_groups),
        grid=(bsz, n_split, rows),
        in_specs=[pl.BlockSpec((GRID_W, dh), lambda b, h, r: (b * rows + r, h)),
                  pl.BlockSpec((s_len, dh), lambda b, h, r: (b, n_split + h)),
                  pl.BlockSpec((s_len, dh), lambda b, h, r: (b, 2 * n_split + h)),
                  pl.BlockSpec((l_len, dh), lambda b, h, r: (ctx_block0 + b, n_split + h)),
                  pl.BlockSpec((l_len, dh), lambda b, h, r: (ctx_block0 + b, 2 * n_split + h)),
                  pl.BlockSpec((1, n_groups, HEADS_PER_LANE_GROUP * GRID_W, kr * GRID_W),
                               lambda b, h, r: (place(r), h, 0, 0))],
        out_specs=pl.BlockSpec((GRID_W, dh), lambda b, h, r: (b * rows + r, h)),
        out_shape=jax.ShapeDtypeStruct((bsz * s_len, d), BF16),
        compiler_params=_params(("arbitrary", "arbitrary", "arbitrary"), 48),
        name="neighbourhood_attention",
    )(qkv, qkv, qkv, qkv, qkv, bias_slabs)


def _dft_cos_sin(n):
    j = jnp.arange(n, dtype=I32)
    ang = ((j[:, None] * j[None, :]) % n).astype(F32) * (2.0 * jnp.pi / n)
    norm = n ** -0.5
    return jnp.cos(ang) * norm, jnp.sin(ang) * norm


def _fnet_stage1_kernel(x_ref, gain_ref, sh_ref, sc_ref, cs_ref, o_ref):
    h = _norm_mod(x_ref[...], gain_ref[...], sh_ref[0], sc_ref[0]).astype(BF16)
    dg = cs_ref.shape[0]
    for g in range(FNET_GROUPS):
        cols = slice(g * dg, (g + 1) * dg)
        a = jnp.dot(h[:, cols], cs_ref[...], preferred_element_type=F32)
        o_ref[0, 0, :, cols] = a[:, :dg].astype(o_ref.dtype)
        o_ref[0, 1, :, cols] = a[:, dg:].astype(o_ref.dtype)


def _fnet_stage1(xt, layout, mod3, nb, layer, gain):
    d = xt.shape[1]
    dg = d // FNET_GROUPS
    tm = layout.tile(512)
    tiles_per_seq = layout.s_len // tm
    cc, sc = _dft_cos_sin(dg)
    cs = jnp.concatenate([cc, sc], axis=1).astype(BF16)
    return pl.pallas_call(
        _fnet_stage1_kernel,
        grid=(layout.n_lat // tm,),
        in_specs=[pl.BlockSpec((tm, d), lambda t: (t, 0)),
                  pl.BlockSpec((1, d), lambda t: (0, 0)),
                  _mod_spec(mod3, layer, 0, nb, layout, tm),
                  _mod_spec(mod3, layer, 1, nb, layout, tm),
                  pl.BlockSpec((dg, 2 * dg), lambda t: (0, 0))],
        out_specs=pl.BlockSpec((1, 2, tm, d), lambda t: (t // tiles_per_seq, 0, t % tiles_per_seq, 0)),
        out_shape=jax.ShapeDtypeStruct((layout.bsz, 2, layout.s_len, d), BF16),
        compiler_params=_params(("arbitrary",), 40),
        name="fnet_channel_dft",
    )(xt, gain.reshape(1, d), mod3, mod3, cs)


def _fnet_stage2_kernel(l_ref, a_ref, o_ref):
    o_ref[...] = jnp.dot(l_ref[...], a_ref[0], preferred_element_type=F32).astype(o_ref.dtype)


def _fnet_stage2(a, s_len):
    bsz, _, _, d = a.shape
    cn, sn = _dft_cos_sin(s_len)
    lhs = jnp.concatenate([cn, -sn], axis=1).astype(BF16)
    tm = min(512, s_len)
    tn = min(1024, d)
    m_tiles = s_len // tm
    return pl.pallas_call(
        _fnet_stage2_kernel,
        grid=(bsz, d // tn, m_tiles),
        in_specs=[pl.BlockSpec((tm, 2 * s_len), lambda b, j, m: (m, 0)),
                  pl.BlockSpec((1, 2 * s_len, tn), lambda b, j, m: (b, 0, j))],
        out_specs=pl.BlockSpec((tm, tn), lambda b, j, m: (b * m_tiles + m, j)),
        out_shape=jax.ShapeDtypeStruct((bsz * s_len, d), BF16),
        compiler_params=_params(("arbitrary", "arbitrary", "arbitrary"), 48),
        name="fnet_position_dft",
    )(lhs, a.reshape(bsz, 2 * s_len, d))


def kernel(x, c, ctx, c_ctx, w_mod, b_mod, norm_gain, final_gain, pool_w, pool_scale, na_w_qkv, na_b_qkv, na_rpb, na_w_o, na_b_o, fnet_w, fnet_b, router_w, router_b, exp_w1, exp_b1, exp_w2, exp_b2):
    bsz, s_len, d = x.shape
    l_len = ctx.shape[1]
    depth = w_mod.shape[0]
    last_ctx_layer = ((depth - 2) // N_MIXERS) * N_MIXERS + 1

    nb = -(-(bsz + 1) // 8) * 8
    c_all = jnp.concatenate([c, c_ctx[None], jnp.zeros((nb - bsz - 1, d), F32)], axis=0)
    mod = _modulation(c_all, w_mod, b_mod)
    mod3 = mod.reshape(depth, nb, N_MOD, d).transpose(0, 2, 1, 3).reshape(depth * N_MOD * nb, 1, d)

    w2_bf16 = exp_w2.astype(BF16)
    lat_only = _Layout(bsz, s_len, l_len, False)
    with_ctx = _Layout(bsz, s_len, l_len, True)
    x_lat, x_ctx = x.reshape(bsz * s_len, d), ctx.reshape(bsz * l_len, d)
    xt = None
    for i in range(depth):
        kind, j = i % N_MIXERS, i // N_MIXERS
        ctx_in = i <= last_ctx_layer
        ctx_out = i < last_ctx_layer
        layout_in = with_ctx if ctx_in else lat_only
        layout_out = with_ctx if ctx_out else lat_only
        if kind == 0:
            if xt is None:
                srcs = (x_lat, x_ctx if ctx_out else None)
            else:
                srcs = (xt, None)
            xt = _pool_mixer(*srcs, layout_out, mod3, nb, i, norm_gain[i, 0], pool_w[j], pool_scale[j])
        elif kind == 1:
            if ctx_in:
                k_fold = jnp.concatenate([jnp.ones((d,), F32), jnp.full((d,), NA_HEAD_DIM ** -0.5 * LOG2_E, F32),
                                          jnp.ones((d,), F32)])
                qkv = _nm_matmul(xt, layout_in, mod3, nb, i, norm_gain[i, 0], na_w_qkv[j] * k_fold,
                                 na_b_qkv[j] * k_fold, 1024, min(1024, d))
                slabs = _na_bias_slabs(na_rpb[j], s_len // GRID_W)
                att = _neighbourhood_attention(qkv, slabs, bsz, s_len, l_len, d)
                xt = _proj_residual(att, na_w_o[j], na_b_o[j], xt, layout_out, mod3, nb, i)
            else:
                raise NotImplementedError("neighbourhood attention without a live context stream")
        else:
            a = _fnet_stage1(xt, layout_out, mod3, nb, i, norm_gain[i, 0])
            f = _fnet_stage2(a, s_len)
            xt = _proj_residual(f, fnet_w[j], fnet_b[j], xt, layout_out, mod3, nb, i)
        xt = _moe(xt, layout_out, layout_out, mod3, nb, i, norm_gain[i, 1], router_w[i], router_b[i],
                  exp_w1, exp_b1, w2_bf16, exp_b2, final_gain if i == depth - 1 else None)
    return xt[:bsz * s_len].reshape(bsz, s_len, d)
```

```python
import functools

import jax
import jax.numpy as jnp
from jax import lax
from jax.experimental import pallas as pl
from jax.experimental.pallas import tpu as pltpu

F32 = jnp.float32
BF16 = jnp.bfloat16
I32 = jnp.int32

GRID_W = 64
N_MIXERS = 3
N_MOD = 6
RMS_EPS = 1e-6
POOL_WINDOWS = (2, 4, 8, 16)
POOL_HALO = 8
NA_HEAD_DIM = 32
NA_ROWS = 8
NA_COLS = 16
NA_KEY_CHUNK = 256
FNET_GROUPS = 4
TOP_K = 4
SWIGLU_LIMIT = 7.0
SWIGLU_ALPHA = 1.702
MOE_BLOCK = 512
LOG2_E = 1.4426950408889634
MASK_VALUE = -1e30
SUBLANES = 8
DMA_PRIORITIES = 2

LANES = 128
HEADS_PER_LANE_GROUP = LANES // NA_HEAD_DIM
MIB = 1024 * 1024


def _params(semantics, vmem_mib):
    return pltpu.CompilerParams(dimension_semantics=semantics, vmem_limit_bytes=vmem_mib * MIB)


def _norm_mod(x, gain, shift, scale):
    ms = jnp.mean(x * x, axis=-1, keepdims=True)
    y = x * lax.rsqrt(ms + RMS_EPS) * gain
    return y * (1.0 + scale) + shift


def _nt_dot(a, b):
    return lax.dot_general(a, b, (((1,), (1,)), ((), ())), preferred_element_type=F32)


def _mod_kernel(c_ref, w_ref, b_ref, o_ref):
    c = c_ref[...]
    s = (c * jax.nn.sigmoid(c)).astype(BF16)
    o_ref[0] = jnp.dot(s, w_ref[0].astype(BF16), preferred_element_type=F32) + b_ref[0]


def _modulation(c_all, w_mod, b_mod):
    depth, d, n = w_mod.shape
    nb = c_all.shape[0]
    tn = 1024
    return pl.pallas_call(
        _mod_kernel,
        grid=(depth, n // tn),
        in_specs=[pl.BlockSpec((nb, d), lambda i, j: (0, 0)),
                  pl.BlockSpec((1, d, tn), lambda i, j: (i, 0, j)),
                  pl.BlockSpec((1, 1, tn), lambda i, j: (i, 0, j))],
        out_specs=pl.BlockSpec((1, nb, tn), lambda i, j: (i, 0, j)),
        out_shape=jax.ShapeDtypeStruct((depth, nb, n), F32),
        compiler_params=_params(("arbitrary", "arbitrary"), 40),
        name="modulation",
    )(c_all, w_mod, b_mod.reshape(depth, 1, n))


class _Layout:
    def __init__(self, bsz, s_len, l_len, with_ctx):
        self.bsz, self.s_len, self.l_len, self.with_ctx = bsz, s_len, l_len, with_ctx
        self.n_lat = bsz * s_len
        self.n_tok = self.n_lat + (bsz * l_len if with_ctx else 0)

    def tile(self, want, seq_local=False):
        tm = min(want, self.s_len)
        if self.with_ctx:
            ctx_span = self.l_len if seq_local else self.bsz * self.l_len
            tm = min(tm, ctx_span)
            assert ctx_span % tm == 0
        assert self.s_len % tm == 0
        return tm

    def mod_row(self, t, tm):
        lat = (t * tm) // self.s_len
        if not self.with_ctx:
            return lat
        return jnp.where(t < self.n_lat // tm, lat, self.bsz)


def _mod_spec(mod3, layer, which, nb, layout, tm):
    base = (layer * N_MOD + which) * nb
    d = mod3.shape[-1]
    return pl.BlockSpec((1, 1, d), lambda t, *_: (base + layout.mod_row(t, tm), 0, 0))


def _pool_kernel(*refs, tm, s_len, l_len, n_lat_tiles, with_ctx, split_src):
    n_src = 6 if split_src else 3
    src_refs, (gain_ref, sh_ref, sc_ref, gt_ref, pw_ref, ps_ref, o_ref, ext_ref) = refs[:n_src], refs[n_src:]
    t = pl.program_id(0)
    if with_ctx:
        is_lat = t < n_lat_tiles
        tiles_per_seq = jnp.where(is_lat, s_len // tm, l_len // tm)
        tile_in_seq = jnp.where(is_lat, t % (s_len // tm), (t - n_lat_tiles) % (l_len // tm))
        seq_len = jnp.where(is_lat, s_len, l_len)
    else:
        tiles_per_seq = s_len // tm
        tile_in_seq = t % (s_len // tm)
        seq_len = s_len
    if split_src:
        xc, xp, xn = (jnp.where(is_lat, a[...], b[...]) for a, b in zip(src_refs[:3], src_refs[3:]))
    else:
        xc, xp, xn = (a[...] for a in src_refs)
    gain, shift, scale = gain_ref[...], sh_ref[0], sc_ref[0]
    hc = _norm_mod(xc, gain, shift, scale)
    hp = _norm_mod(xp, gain, shift, scale)
    hn = _norm_mod(xn, gain, shift, scale)
    ext_ref[0:POOL_HALO] = jnp.where(tile_in_seq > 0, hp, 0.0)
    ext_ref[POOL_HALO:POOL_HALO + tm] = hc
    ext_ref[POOL_HALO + tm:2 * POOL_HALO + tm] = jnp.where(tile_in_seq < tiles_per_seq - 1, hn, 0.0)
    pos = tile_in_seq * tm + lax.broadcasted_iota(I32, (tm, 1), 0)
    dg = hc.shape[1] // len(POOL_WINDOWS)
    gate, pscale = gt_ref[0], ps_ref[...]
    for g, w in enumerate(POOL_WINDOWS):
        cs = slice(g * dg, (g + 1) * dg)
        start = POOL_HALO - w // 2
        acc = ext_ref[start:start + tm, cs]
        for j in range(1, w):
            acc = acc + ext_ref[start + j:start + j + tm, cs]
        cnt = jnp.minimum(pos + w // 2, seq_len) - jnp.maximum(pos - w // 2, 0)
        diff = (acc / cnt.astype(F32) - hc[:, cs]).astype(BF16)
        y = jnp.dot(diff, pw_ref[g], preferred_element_type=F32)
        o_ref[:, cs] = xc[:, cs] + gate[:, cs] * (y * pscale[:, cs])


def _pool_mixer(x_lat, x_ctx, layout, mod3, nb, layer, gain, pool_w, pool_scale):
    d = x_lat.shape[1]
    tm = layout.tile(256, seq_local=True)
    hb = tm // POOL_HALO
    n_lat_tiles = layout.n_lat // tm
    split_src = x_ctx is not None

    def src_specs(n_rows, first_tile):
        def local(t):
            return jnp.clip(t - first_tile, 0, n_rows // tm - 1)
        return [pl.BlockSpec((tm, d), lambda t: (local(t), 0)),
                pl.BlockSpec((POOL_HALO, d), lambda t: (jnp.maximum(local(t) * hb - 1, 0), 0)),
                pl.BlockSpec((POOL_HALO, d), lambda t: (jnp.minimum((local(t) + 1) * hb, n_rows // POOL_HALO - 1), 0))]

    specs = src_specs(x_lat.shape[0], 0)
    srcs = [x_lat] * 3
    if split_src:
        specs += src_specs(x_ctx.shape[0], n_lat_tiles)
        srcs += [x_ctx] * 3
    kern = functools.partial(_pool_kernel, tm=tm, s_len=layout.s_len, l_len=layout.l_len,
                             n_lat_tiles=n_lat_tiles, with_ctx=layout.with_ctx, split_src=split_src)
    return pl.pallas_call(
        kern,
        grid=(layout.n_tok // tm,),
        in_specs=specs + [
                  pl.BlockSpec((1, d), lambda t: (0, 0)),
                  _mod_spec(mod3, layer, 0, nb, layout, tm),
                  _mod_spec(mod3, layer, 1, nb, layout, tm),
                  _mod_spec(mod3, layer, 2, nb, layout, tm),
                  pl.BlockSpec(pool_w.shape, lambda t: (0, 0, 0)),
                  pl.BlockSpec((1, d), lambda t: (0, 0))],
        out_specs=pl.BlockSpec((tm, d), lambda t: (t, 0)),
        out_shape=jax.ShapeDtypeStruct((layout.n_tok, d), F32),
        scratch_shapes=[pltpu.VMEM((tm + 2 * POOL_HALO, d), F32)],
        compiler_params=_params(("arbitrary",), 40),
        name="pool_mixer",
    )(*srcs, gain.reshape(1, d), mod3, mod3, mod3, pool_w.astype(BF16), pool_scale.reshape(1, d))


def _router_kernel(x_ref, gain_ref, sh_ref, sc_ref, wh_ref, wl_ref, rb_ref, h_ref, idx_ref, gate_ref):
    h = _norm_mod(x_ref[...], gain_ref[...], sh_ref[0], sc_ref[0])
    h_ref[...] = h
    h_hi = h.astype(BF16)
    h_lo = (h - h_hi.astype(F32)).astype(BF16)
    wh, wl = wh_ref[...], wl_ref[...]
    logits = _nt_dot(wh, h_hi) + _nt_dot(wh, h_lo) + _nt_dot(wl, h_hi) + rb_ref[...]
    n_exp = logits.shape[0]
    e_iota = lax.broadcasted_iota(I32, logits.shape, 0)
    work = logits
    vals, idxs = [], []
    for _ in range(TOP_K):
        m = jnp.max(work, axis=0, keepdims=True)
        idx = jnp.min(jnp.where(work == m, e_iota, n_exp), axis=0, keepdims=True)
        work = jnp.where(e_iota == idx, -jnp.inf, work)
        vals.append(m)
        idxs.append(idx)
    exps = [jnp.exp(v - vals[0]) for v in vals]
    denom = exps[0] + exps[1] + exps[2] + exps[3]
    pad_rows = idx_ref.shape[0] - TOP_K
    idx_ref[...] = jnp.concatenate(idxs + [jnp.zeros((pad_rows, logits.shape[1]), I32)], axis=0)
    gate_ref[...] = jnp.concatenate([e / denom for e in exps] + [jnp.zeros((pad_rows, logits.shape[1]), F32)],
                                    axis=0)


def _router(xt, layout, mod3, nb, layer, gain, router_w, router_b):
    n_tok, d = xt.shape
    n_exp = router_w.shape[1]
    tm = layout.tile(512)
    wt = router_w.T
    w_hi = wt.astype(BF16)
    w_lo = (wt - w_hi.astype(F32)).astype(BF16)
    return pl.pallas_call(
        _router_kernel,
        grid=(layout.n_tok // tm,),
        in_specs=[pl.BlockSpec((tm, d), lambda t: (t, 0)),
                  pl.BlockSpec((1, d), lambda t: (0, 0)),
                  _mod_spec(mod3, layer, 3, nb, layout, tm),
                  _mod_spec(mod3, layer, 4, nb, layout, tm),
                  pl.BlockSpec((n_exp, d), lambda t: (0, 0)),
                  pl.BlockSpec((n_exp, d), lambda t: (0, 0)),
                  pl.BlockSpec((n_exp, 1), lambda t: (0, 0))],
        out_specs=[pl.BlockSpec((tm, d), lambda t: (t, 0)),
                   pl.BlockSpec((8, tm), lambda t: (0, t)),
                   pl.BlockSpec((8, tm), lambda t: (0, t))],
        out_shape=[jax.ShapeDtypeStruct((layout.n_tok, d), F32),
                   jax.ShapeDtypeStruct((8, layout.n_tok), I32),
                   jax.ShapeDtypeStruct((8, layout.n_tok), F32)],
        compiler_params=_params(("arbitrary",), 40),
        name="moe_router",
    )(xt, gain.reshape(1, d), mod3, mod3, w_hi, w_lo, router_b.reshape(n_exp, 1))


def _rank_kernel(idx_ref, rank_ref, cnt_ref, run_ref, *, n_exp):
    t = pl.program_id(0)

    @pl.when(t == 0)
    def _():
        run_ref[...] = jnp.zeros_like(run_ref)

    tm = idx_ref.shape[1]
    e_iota = lax.broadcasted_iota(I32, (n_exp, tm), 0)
    sel = [e_iota == idx_ref[k:k + 1, :] for k in range(TOP_K)]
    member = sel[0] | sel[1] | sel[2] | sel[3]
    member_f = jnp.where(member, 1.0, 0.0)
    strictly_before = (lax.broadcasted_iota(I32, (tm, tm), 0) < lax.broadcasted_iota(I32, (tm, tm), 1))
    prefix = jnp.dot(member_f.astype(BF16), jnp.where(strictly_before, 1.0, 0.0).astype(BF16),
                     preferred_element_type=F32)
    base = run_ref[:, 0:1] + prefix
    ranks = [jnp.sum(jnp.where(s, base, 0.0), axis=0, keepdims=True).astype(I32) for s in sel]
    pad_rows = rank_ref.shape[0] - TOP_K
    rank_ref[...] = jnp.concatenate(ranks + [jnp.zeros((pad_rows, tm), I32)], axis=0)
    run_ref[...] = run_ref[...] + jnp.sum(member_f, axis=1, keepdims=True)
    cnt_ref[...] = run_ref[...].astype(I32)


def _ranks(idx_t, n_exp):
    n_tok = idx_t.shape[1]
    tm = 512 if n_tok % 512 == 0 else 256
    return pl.pallas_call(
        functools.partial(_rank_kernel, n_exp=n_exp),
        grid=(n_tok // tm,),
        in_specs=[pl.BlockSpec((8, tm), lambda t: (0, t))],
        out_specs=[pl.BlockSpec((8, tm), lambda t: (0, t)),
                   pl.BlockSpec((n_exp, LANES), lambda t: (0, 0))],
        out_shape=[jax.ShapeDtypeStruct((8, n_tok), I32),
                   jax.ShapeDtypeStruct((n_exp, LANES), I32)],
        scratch_shapes=[pltpu.VMEM((n_exp, LANES), F32)],
        compiler_params=_params(("arbitrary",), 32),
        name="moe_rank",
    )(idx_t)


def _dest_kernel(idx_ref, rank_ref, ps_ref, dest_ref, *, n_exp):
    tm = idx_ref.shape[1]
    e_iota = lax.broadcasted_iota(I32, (n_exp, tm), 0)
    ps = ps_ref[:, 0:1]
    rows = []
    for k in range(TOP_K):
        start = jnp.sum(jnp.where(e_iota == idx_ref[k:k + 1, :], ps, 0.0), axis=0, keepdims=True)
        rows.append(start.astype(I32) + rank_ref[k:k + 1, :])
    pad_rows = dest_ref.shape[0] - TOP_K
    dest_ref[...] = jnp.concatenate(rows + [jnp.zeros((pad_rows, tm), I32)], axis=0)


def _dests(idx_t, rank_t, pad_starts):
    n_tok = idx_t.shape[1]
    n_exp = pad_starts.shape[0]
    tm = 512 if n_tok % 512 == 0 else 256
    ps = jnp.broadcast_to(pad_starts.astype(F32)[:, None], (n_exp, LANES))
    return pl.pallas_call(
        functools.partial(_dest_kernel, n_exp=n_exp),
        grid=(n_tok // tm,),
        in_specs=[pl.BlockSpec((8, tm), lambda t: (0, t)),
                  pl.BlockSpec((8, tm), lambda t: (0, t)),
                  pl.BlockSpec((n_exp, LANES), lambda t: (0, 0))],
        out_specs=pl.BlockSpec((8, tm), lambda t: (0, t)),
        out_shape=jax.ShapeDtypeStruct((8, n_tok), I32),
        compiler_params=_params(("arbitrary",), 32),
        name="moe_dest",
    )(idx_t, rank_t, ps)


def _tile_major(dest_t, tm):
    n_tok = dest_t.shape[1]
    return dest_t[:TOP_K].reshape(TOP_K, n_tok // tm, tm).transpose(1, 0, 2).reshape(-1)


def _dispatch_kernel(cnt_ref, ps_ref, h_ref, dest_ref, xs_ref, zero_ref, sem, zsem, *, n_exp, block):
    t = pl.program_id(0)
    tm = h_ref.shape[0]

    def row_copy(r, d):
        return pltpu.make_async_copy(h_ref.at[pl.ds(r, 1)], xs_ref.at[pl.ds(d, 1)], sem)

    def issue(r8, carry):
        for u in range(SUBLANES):
            r = r8 * SUBLANES + u
            for k in range(TOP_K):
                row_copy(r, dest_ref[k * tm + r]).start(priority=k % DMA_PRIORITIES)
        return carry

    lax.fori_loop(0, tm // SUBLANES, issue, 0)

    @pl.when(t == 0)
    def _():
        zero_ref[...] = jnp.zeros_like(zero_ref)

        def zero_copy(d):
            return pltpu.make_async_copy(zero_ref.at[pl.ds(0, 1)], xs_ref.at[pl.ds(d, 1)], zsem)

        for e in range(n_exp):
            cnt = cnt_ref[e]
            first = ps_ref[e] + cnt
            n_pad = (-cnt) % block

            def zissue(i, carry, first=first):
                zero_copy(first + i).start()
                return carry

            def zwait(i, carry):
                zero_copy(0).wait()
                return carry

            lax.fori_loop(0, n_pad, zissue, 0)
            lax.fori_loop(0, n_pad, zwait, 0)

        def tail_copy(j):
            return pltpu.make_async_copy(zero_ref, xs_ref.at[pl.ds(j * block, block)], zsem)

        def tail_issue(j, carry):
            tail_copy(j).start()
            return carry

        def tail_wait(j, carry):
            tail_copy(j).wait()
            return carry

        first_tail = ps_ref[n_exp] // block
        lax.fori_loop(first_tail, xs_ref.shape[0] // block, tail_issue, 0)
        lax.fori_loop(first_tail, xs_ref.shape[0] // block, tail_wait, 0)

    for _ in range(TOP_K):
        pltpu.make_async_copy(zero_ref.at[pl.ds(0, tm)], xs_ref.at[pl.ds(0, tm)], sem).wait()


def _dispatch(h, dest_flat, counts, pad_starts, n_slots, tm):
    n_tok, d = h.shape
    n_exp = counts.shape[0]
    grid_spec = pltpu.PrefetchScalarGridSpec(
        num_scalar_prefetch=2,
        grid=(n_tok // tm,),
        in_specs=[pl.BlockSpec((tm, d), lambda t, *_: (t, 0)),
                  pl.BlockSpec((TOP_K * tm,), lambda t, *_: (t,), memory_space=pltpu.SMEM)],
        out_specs=pl.BlockSpec(memory_space=pl.ANY),
        scratch_shapes=[pltpu.VMEM((MOE_BLOCK, d), F32), pltpu.SemaphoreType.DMA, pltpu.SemaphoreType.DMA],
    )
    return pl.pallas_call(
        functools.partial(_dispatch_kernel, n_exp=n_exp, block=MOE_BLOCK),
        grid_spec=grid_spec,
        out_shape=jax.ShapeDtypeStruct((n_slots, d), F32),
        compiler_params=_params(("arbitrary",), 32),
        name="moe_dispatch",
    )(counts, pad_starts, h, dest_flat)


def _expert_kernel(be_ref, nu_ref, nv_ref, x_ref, w1_ref, b1_ref, w2_ref, b2_ref, y_ref):
    n_valid = nv_ref[pl.program_id(0)]
    half = x_ref.shape[0] // 2

    def ffn(x):
        f = w2_ref.shape[2]
        hg = jnp.dot(x.astype(BF16), w1_ref[0, 0].astype(BF16), preferred_element_type=F32) + b1_ref[0, 0]
        gate = jnp.minimum(hg[:, :f], SWIGLU_LIMIT)
        up = jnp.clip(hg[:, f:], -SWIGLU_LIMIT, SWIGLU_LIMIT)
        glu = gate * jax.nn.sigmoid(SWIGLU_ALPHA * gate)
        act = (glu * (up + 1.0)).astype(BF16)
        return jnp.dot(act, w2_ref[0, 0], preferred_element_type=F32) + b2_ref[0, 0]

    @pl.when(n_valid > half)
    def _():
        y_ref[...] = ffn(x_ref[...])

    @pl.when((n_valid > 0) & (n_valid <= half))
    def _():
        y_ref[0:half] = ffn(x_ref[0:half])
        y_ref[half:] = jnp.zeros((x_ref.shape[0] - half, y_ref.shape[1]), y_ref.dtype)

    @pl.when(n_valid == 0)
    def _():
        y_ref[...] = jnp.zeros_like(y_ref)


def _experts(xs, block_expert, n_used, n_valid, w1, b1, w2, b2, layer):
    n_slots, d = xs.shape
    depth, n_exp, _, f2 = w1.shape
    f = w2.shape[2]
    bm = MOE_BLOCK

    def row_block(j, be, nu, nv):
        return (jnp.minimum(j, nu[0] - 1), 0)

    def expert_block(j, be, nu, nv):
        return (layer, be[j], 0, 0)

    grid_spec = pltpu.PrefetchScalarGridSpec(
        num_scalar_prefetch=3,
        grid=(n_slots // bm,),
        in_specs=[pl.BlockSpec((bm, d), row_block),
                  pl.BlockSpec((1, 1, d, f2), expert_block),
                  pl.BlockSpec((1, 1, 1, f2), expert_block),
                  pl.BlockSpec((1, 1, f, d), expert_block),
                  pl.BlockSpec((1, 1, 1, d), expert_block)],
        out_specs=pl.BlockSpec((bm, d), lambda j, be, nu, nv: (j, 0)),
    )
    return pl.pallas_call(
        _expert_kernel,
        grid_spec=grid_spec,
        out_shape=jax.ShapeDtypeStruct((n_slots, d), F32),
        compiler_params=_params(("arbitrary",), 56),
        name="moe_experts",
    )(block_expert, n_used, n_valid, xs, w1, b1.reshape(depth, n_exp, 1, f2), w2, b2.reshape(depth, n_exp, 1, d))


def _combine_kernel(x_ref, g_ref, dest_ref, gt_ref, fg_ref, ys_ref, o_ref, buf_ref, sem, *, final_norm):
    tm = x_ref.shape[0]

    def row_copy(k, r8, u, d):
        return pltpu.make_async_copy(ys_ref.at[pl.ds(d, 1)], buf_ref.at[k, r8, pl.ds(u, 1)], sem)

    def issue(r8, carry):
        for u in range(SUBLANES):
            for k in range(TOP_K):
                row_copy(k, r8, u, dest_ref[k * tm + r8 * SUBLANES + u]).start(priority=k % DMA_PRIORITIES)
        return carry

    lax.fori_loop(0, tm // SUBLANES, issue, 0)

    g8 = g_ref[...]
    g_cols = jnp.concatenate([g8, jnp.zeros((tm - 8, tm), F32)], axis=0).T

    for k in range(TOP_K):
        pltpu.make_async_copy(ys_ref.at[pl.ds(0, tm)], o_ref, sem).wait()

    d = x_ref.shape[1]
    acc = g_cols[:, 0:1] * buf_ref[0].reshape(tm, d)
    for k in range(1, TOP_K):
        acc = acc + g_cols[:, k:k + 1] * buf_ref[k].reshape(tm, d)
    out = x_ref[...] + gt_ref[0] * acc
    if final_norm:
        ms = jnp.mean(out * out, axis=-1, keepdims=True)
        out = out * lax.rsqrt(ms + RMS_EPS) * fg_ref[...]
    o_ref[...] = out


def _combine(xt, out_layout, gates_t, dest_flat, ys, mod3, nb, layer, final_gain, tm):
    d = xt.shape[1]
    final_norm = final_gain is not None
    fg = (final_gain if final_norm else jnp.ones((d,), F32)).reshape(1, d)
    return pl.pallas_call(
        functools.partial(_combine_kernel, final_norm=final_norm),
        grid=(out_layout.n_tok // tm,),
        in_specs=[pl.BlockSpec((tm, d), lambda t: (t, 0)),
                  pl.BlockSpec((8, tm), lambda t: (0, t)),
                  pl.BlockSpec((TOP_K * tm,), lambda t: (t,), memory_space=pltpu.SMEM),
                  _mod_spec(mod3, layer, 5, nb, out_layout, tm),
                  pl.BlockSpec((1, d), lambda t: (0, 0)),
                  pl.BlockSpec(memory_space=pl.ANY)],
        out_specs=pl.BlockSpec((tm, d), lambda t: (t, 0)),
        out_shape=jax.ShapeDtypeStruct((out_layout.n_tok, d), F32),
        scratch_shapes=[pltpu.VMEM((TOP_K, tm // SUBLANES, SUBLANES, d), F32), pltpu.SemaphoreType.DMA],
        compiler_params=_params(("arbitrary",), 32),
        name="moe_combine",
    )(xt, gates_t, dest_flat, mod3, fg, ys)


def _moe(xt, layout, out_layout, mod3, nb, layer, gain, router_w, router_b, w1, b1, w2, b2, final_gain):
    n_exp = router_w.shape[1]
    h, idx_t, gates_t = _router(xt, layout, mod3, nb, layer, gain, router_w, router_b)
    rank_t, cnt = _ranks(idx_t, n_exp)
    counts = cnt[:, 0]
    padded = (counts + MOE_BLOCK - 1) // MOE_BLOCK * MOE_BLOCK
    pad_ends = jnp.cumsum(padded)
    pad_starts = pad_ends - padded
    n_blocks = -(-layout.n_tok * TOP_K // MOE_BLOCK) + n_exp
    n_slots = n_blocks * MOE_BLOCK
    block_first_row = jnp.arange(n_blocks, dtype=I32) * MOE_BLOCK
    block_expert = jnp.minimum(jnp.sum(pad_ends[None, :] <= block_first_row[:, None], axis=1), n_exp - 1).astype(I32)
    n_used = (pad_ends[-1:] // MOE_BLOCK).astype(I32)
    dest_t = _dests(idx_t, rank_t, pad_starts)
    tm_d = layout.tile(256)
    tm_c = tm_d
    slot_bounds = jnp.concatenate([pad_starts, pad_ends[-1:]]).astype(I32)
    xs = _dispatch(h, _tile_major(dest_t, tm_d), counts, slot_bounds, n_slots, tm_d)
    rows_before = block_first_row - pad_starts[block_expert]
    n_valid = jnp.clip(counts[block_expert] - rows_before, 0, MOE_BLOCK).astype(I32)
    ys = _experts(xs, block_expert, n_used, n_valid, w1, b1, w2, b2, layer)
    return _combine(xt, out_layout, gates_t, _tile_major(dest_t, tm_c), ys, mod3, nb, layer, final_gain, tm_c)


def _nm_matmul_kernel(x_ref, gain_ref, sh_ref, sc_ref, w_ref, b_ref, o_ref, h_ref):
    @pl.when(pl.program_id(1) == 0)
    def _():
        h_ref[...] = _norm_mod(x_ref[...], gain_ref[...], sh_ref[0], sc_ref[0]).astype(BF16)

    o_ref[...] = (jnp.dot(h_ref[...], w_ref[...], preferred_element_type=F32) + b_ref[...]).astype(o_ref.dtype)


def _nm_matmul(xt, layout, mod3, nb, layer, gain, w, b, tm_want, tn):
    d = xt.shape[1]
    n = w.shape[1]
    tm = layout.tile(tm_want)
    return pl.pallas_call(
        _nm_matmul_kernel,
        grid=(layout.n_tok // tm, n // tn),
        in_specs=[pl.BlockSpec((tm, d), lambda m, j: (m, 0)),
                  pl.BlockSpec((1, d), lambda m, j: (0, 0)),
                  _mod_spec(mod3, layer, 0, nb, layout, tm),
                  _mod_spec(mod3, layer, 1, nb, layout, tm),
                  pl.BlockSpec((d, tn), lambda m, j: (0, j)),
                  pl.BlockSpec((1, tn), lambda m, j: (0, j))],
        out_specs=pl.BlockSpec((tm, tn), lambda m, j: (m, j)),
        out_shape=jax.ShapeDtypeStruct((layout.n_tok, n), BF16),
        scratch_shapes=[pltpu.VMEM((tm, d), BF16)],
        compiler_params=_params(("arbitrary", "arbitrary"), 48),
        name="norm_mod_matmul",
    )(xt, gain.reshape(1, d), mod3, mod3, w.astype(BF16), b.reshape(1, n))


def _proj_residual_kernel(a_ref, w_ref, b_ref, x_ref, gt_ref, o_ref):
    y = jnp.dot(a_ref[...], w_ref[...], preferred_element_type=F32) + b_ref[...]
    o_ref[...] = x_ref[...] + gt_ref[0] * y


def _proj_residual(a, w, b, xt, out_layout, mod3, nb, layer, tm=None, a_tile=lambda m: m):
    k, n = w.shape
    tm = out_layout.tile(512) if tm is None else tm
    assert out_layout.s_len % tm == 0
    return pl.pallas_call(
        _proj_residual_kernel,
        grid=(out_layout.n_tok // tm,),
        in_specs=[pl.BlockSpec((tm, k), lambda m: (a_tile(m), 0)),
                  pl.BlockSpec((k, n), lambda m: (0, 0)),
                  pl.BlockSpec((1, n), lambda m: (0, 0)),
                  pl.BlockSpec((tm, n), lambda m: (m, 0)),
                  _mod_spec(mod3, layer, 2, nb, out_layout, tm)],
        out_specs=pl.BlockSpec((tm, n), lambda m: (m, 0)),
        out_shape=jax.ShapeDtypeStruct((out_layout.n_tok, n), F32),
        compiler_params=_params(("arbitrary",), 48),
        name="proj_residual",
    )(a, w.astype(BF16), b.reshape(1, n), xt, mod3)


def _na_bias_slabs(rpb, rows):
    n_heads = rpb.shape[0]
    kr = min(NA_ROWS, rows)
    col = jnp.arange(GRID_W)
    c0 = jnp.clip(col - NA_COLS // 2, 0, GRID_W - NA_COLS)
    col_mask = (col[None, :] >= c0[:, None]) & (col[None, :] < c0[:, None] + NA_COLS)
    dc_idx = jnp.clip(col[None, :] - col[:, None] + NA_COLS - 1, 0, 2 * NA_COLS - 2)
    n_place = NA_ROWS
    dr = jnp.arange(n_place)[:, None] + jnp.arange(kr)[None, :]
    bias = rpb[:, dr][:, :, :, dc_idx]
    bias = jnp.where(col_mask[None, None, None], bias * LOG2_E, MASK_VALUE)
    bias = jnp.transpose(bias, (1, 0, 3, 2, 4))
    return bias.reshape(n_place, n_heads // HEADS_PER_LANE_GROUP, HEADS_PER_LANE_GROUP * GRID_W, kr * GRID_W)


def _na_kernel(q_ref, k_ref, v_ref, kc_ref, vc_ref, bias_ref, o_ref, *, rows, kr, n_groups):
    r = pl.program_id(2)
    r0 = jnp.clip(r - kr // 2, 0, rows - kr)
    base = pl.multiple_of(r0 * GRID_W, GRID_W)
    n_lat, n_ctx = kr * GRID_W, kc_ref.shape[0]
    ch = NA_KEY_CHUNK
    nq = HEADS_PER_LANE_GROUP * GRID_W
    head_of_row = lax.broadcasted_iota(I32, (nq, LANES), 0) // GRID_W
    head_of_lane = lax.broadcasted_iota(I32, (nq, LANES), 1) // NA_HEAD_DIM
    own = head_of_row == head_of_lane
    for g in range(n_groups):
        gs = slice(g * LANES, (g + 1) * LANES)
        qg = q_ref[:, gs]
        qbd = jnp.where(own, jnp.concatenate([qg] * HEADS_PER_LANE_GROUP, axis=0), jnp.zeros((), BF16))
        m = l = acc = None
        for c in range((n_lat + n_ctx) // ch):
            if c < n_lat // ch:
                rows_c = pl.ds(base + c * ch, ch)
                s = _nt_dot(qbd, k_ref[rows_c, gs]) + bias_ref[0, g, :, c * ch:(c + 1) * ch]
                vals = v_ref[rows_c, gs]
            else:
                rows_c = slice(c * ch - n_lat, (c + 1) * ch - n_lat)
                s = _nt_dot(qbd, kc_ref[rows_c, gs])
                vals = vc_ref[rows_c, gs]
            m_c = jnp.max(s, axis=-1, keepdims=True)
            if m is None:
                m = m_c
                p = jnp.exp2(s - m)
                l = jnp.sum(p, axis=-1, keepdims=True)
                acc = jnp.dot(p.astype(BF16), vals, preferred_element_type=F32)
            else:
                m_new = jnp.maximum(m, m_c)
                alpha = jnp.exp2(m - m_new)
                p = jnp.exp2(s - m_new)
                l = l * alpha + jnp.sum(p, axis=-1, keepdims=True)
                acc = acc * alpha + jnp.dot(p.astype(BF16), vals, preferred_element_type=F32)
                m = m_new
        o = jnp.where(own, acc / l, 0.0)
        og = o[0:GRID_W]
        for h in range(1, HEADS_PER_LANE_GROUP):
            og = og + o[h * GRID_W:(h + 1) * GRID_W]
        o_ref[:, gs] = og.astype(o_ref.dtype)


def _neighbourhood_attention(qkv, bias_slabs, bsz, s_len, l_len, d):
    rows = s_len // GRID_W
    kr = min(NA_ROWS, rows)
    n_split = 1
    dh = d // n_split
    n_groups = dh // LANES
    ctx_block0 = bsz * s_len // l_len

    def place(r):
        r0 = jnp.clip(r - kr // 2, 0, rows - kr)
        return r0 - r + NA_ROWS - 1

    return pl.pallas_call(
        functools.partial(_na_kernel, rows=rows, kr=kr, n_groups=n_groups),
        grid=(bsz, n_split, rows),
        in_specs=[pl.BlockSpec((GRID_W, dh), lambda b, h, r: (b * rows + r, h)),
                  pl.BlockSpec((s_len, dh), lambda b, h, r: (b, n_split + h)),
                  pl.BlockSpec((s_len, dh), lambda b, h, r: (b, 2 * n_split + h)),
                  pl.BlockSpec((l_len, dh), lambda b, h, r: (ctx_block0 + b, n_split + h)),
                  pl.BlockSpec((l_len, dh), lambda b, h, r: (ctx_block0 + b, 2 * n_split + h)),
                  pl.BlockSpec((1, n_groups, HEADS_PER_LANE_GROUP * GRID_W, kr * GRID_W),
                               lambda b, h, r: (place(r), h, 0, 0))],
        out_specs=pl.BlockSpec((GRID_W, dh), lambda b, h, r: (b * rows + r, h)),
        out_shape=jax.ShapeDtypeStruct((bsz * s_len, d), BF16),
        compiler_params=_params(("arbitrary", "arbitrary", "arbitrary"), 60),
        name="neighbourhood_attention",
    )(qkv, qkv, qkv, qkv, qkv, bias_slabs)


def _dft_cos_sin(n):
    j = jnp.arange(n, dtype=I32)
    ang = ((j[:, None] * j[None, :]) % n).astype(F32) * (2.0 * jnp.pi / n)
    norm = n ** -0.5
    return jnp.cos(ang) * norm, jnp.sin(ang) * norm


def _fnet_stage1_kernel(x_ref, gain_ref, sh_ref, sc_ref, cs_ref, o_ref):
    h = _norm_mod(x_ref[...], gain_ref[...], sh_ref[0], sc_ref[0]).astype(BF16)
    dg = cs_ref.shape[0]
    for g in range(FNET_GROUPS):
        cols = slice(g * dg, (g + 1) * dg)
        a = jnp.dot(h[:, cols], cs_ref[...], preferred_element_type=F32)
        o_ref[0, 0, :, cols] = a[:, :dg].astype(o_ref.dtype)
        o_ref[0, 1, :, cols] = a[:, dg:].astype(o_ref.dtype)


def _fnet_stage1(xt, layout, mod3, nb, layer, gain):
    d = xt.shape[1]
    dg = d // FNET_GROUPS
    tm = layout.tile(512)
    tiles_per_seq = layout.s_len // tm
    cc, sc = _dft_cos_sin(dg)
    cs = jnp.concatenate([cc, sc], axis=1).astype(BF16)
    return pl.pallas_call(
        _fnet_stage1_kernel,
        grid=(layout.n_lat // tm,),
        in_specs=[pl.BlockSpec((tm, d), lambda t: (t, 0)),
                  pl.BlockSpec((1, d), lambda t: (0, 0)),
                  _mod_spec(mod3, layer, 0, nb, layout, tm),
                  _mod_spec(mod3, layer, 1, nb, layout, tm),
                  pl.BlockSpec((dg, 2 * dg), lambda t: (0, 0))],
        out_specs=pl.BlockSpec((1, 2, tm, d), lambda t: (t // tiles_per_seq, 0, t % tiles_per_seq, 0)),
        out_shape=jax.ShapeDtypeStruct((layout.bsz, 2, layout.s_len, d), BF16),
        compiler_params=_params(("arbitrary",), 40),
        name="fnet_channel_dft",
    )(xt, gain.reshape(1, d), mod3, mod3, cs)


def _fnet_stage2_kernel(lc_ref, ls_ref, rev_ref, ac_ref, as_ref, o_ref):
    tm = o_ref.shape[2]
    p = jnp.dot(lc_ref[0], ac_ref[0, 0], preferred_element_type=F32)
    q = jnp.dot(ls_ref[0], as_ref[0, 0], preferred_element_type=F32)
    o_ref[0, 0] = (p - q)[:tm].astype(o_ref.dtype)
    z = (p + q).astype(BF16)
    o_ref[0, 1] = jnp.dot(rev_ref[...], z, preferred_element_type=F32).astype(o_ref.dtype)


def _fnet_stage2(a, s_len):
    bsz, _, _, d = a.shape
    cn, sn = _dft_cos_sin(s_len)
    half = s_len // 2
    tm = min(512, half)
    tn = min(1024, d)
    m_tiles = half // tm
    ext = tm + SUBLANES
    rows = (jnp.arange(m_tiles)[:, None] * tm + jnp.arange(ext)[None, :]) % s_len
    lhs_cos, lhs_sin = cn[rows].astype(BF16), sn[rows].astype(BF16)
    rev = (jnp.arange(ext)[None, :] == tm - jnp.arange(tm)[:, None]).astype(BF16)
    out = pl.pallas_call(
        _fnet_stage2_kernel,
        grid=(bsz, d // tn, m_tiles),
        in_specs=[pl.BlockSpec((1, ext, s_len), lambda b, j, m: (m, 0, 0)),
                  pl.BlockSpec((1, ext, s_len), lambda b, j, m: (m, 0, 0)),
                  pl.BlockSpec((tm, ext), lambda b, j, m: (0, 0)),
                  pl.BlockSpec((1, 1, s_len, tn), lambda b, j, m: (b, 0, 0, j)),
                  pl.BlockSpec((1, 1, s_len, tn), lambda b, j, m: (b, 1, 0, j))],
        out_specs=pl.BlockSpec((1, 2, tm, tn), lambda b, j, m: (b, 0, m, j)),
        out_shape=jax.ShapeDtypeStruct((bsz, 2, half, d), BF16),
        compiler_params=_params(("arbitrary", "arbitrary", "arbitrary"), 48),
        name="fnet_position_dft",
    )(lhs_cos, lhs_sin, rev, a, a)

    def stored_tile(t):
        b, w = t // (2 * m_tiles), t % (2 * m_tiles)
        return b * 2 * m_tiles + jnp.where(w < m_tiles, w, 3 * m_tiles - 1 - w)

    return out.reshape(bsz * s_len, d), tm, stored_tile


def kernel(x, c, ctx, c_ctx, w_mod, b_mod, norm_gain, final_gain, pool_w, pool_scale, na_w_qkv, na_b_qkv, na_rpb, na_w_o, na_b_o, fnet_w, fnet_b, router_w, router_b, exp_w1, exp_b1, exp_w2, exp_b2):
    bsz, s_len, d = x.shape
    l_len = ctx.shape[1]
    depth = w_mod.shape[0]
    last_ctx_layer = ((depth - 2) // N_MIXERS) * N_MIXERS + 1

    nb = -(-(bsz + 1) // 8) * 8
    c_all = jnp.concatenate([c, c_ctx[None], jnp.zeros((nb - bsz - 1, d), F32)], axis=0)
    mod = _modulation(c_all, w_mod, b_mod)
    mod3 = mod.reshape(depth, nb, N_MOD, d).transpose(0, 2, 1, 3).reshape(depth * N_MOD * nb, 1, d)

    w2_bf16 = exp_w2.astype(BF16)
    lat_only = _Layout(bsz, s_len, l_len, False)
    with_ctx = _Layout(bsz, s_len, l_len, True)
    x_lat, x_ctx = x.reshape(bsz * s_len, d), ctx.reshape(bsz * l_len, d)
    xt = None
    for i in range(depth):
        kind, j = i % N_MIXERS, i // N_MIXERS
        ctx_in = i <= last_ctx_layer
        ctx_out = i < last_ctx_layer
        layout_in = with_ctx if ctx_in else lat_only
        layout_out = with_ctx if ctx_out else lat_only
        if kind == 0:
            if xt is None:
                srcs = (x_lat, x_ctx if ctx_out else None)
            else:
                srcs = (xt, None)
            xt = _pool_mixer(*srcs, layout_out, mod3, nb, i, norm_gain[i, 0], pool_w[j], pool_scale[j])
        elif kind == 1:
            if ctx_in:
                k_fold = jnp.concatenate([jnp.ones((d,), F32), jnp.full((d,), NA_HEAD_DIM ** -0.5 * LOG2_E, F32),
                                          jnp.ones((d,), F32)])
                qkv = _nm_matmul(xt, layout_in, mod3, nb, i, norm_gain[i, 0], na_w_qkv[j] * k_fold,
                                 na_b_qkv[j] * k_fold, 1024, min(1024, d))
                slabs = _na_bias_slabs(na_rpb[j], s_len // GRID_W)
                att = _neighbourhood_attention(qkv, slabs, bsz, s_len, l_len, d)
                xt = _proj_residual(att, na_w_o[j], na_b_o[j], xt, layout_out, mod3, nb, i)
            else:
                raise NotImplementedError("neighbourhood attention without a live context stream")
        else:
            a = _fnet_stage1(xt, layout_out, mod3, nb, i, norm_gain[i, 0])
            f, f_tile, stored_tile = _fnet_stage2(a, s_len)
            xt = _proj_residual(f, fnet_w[j], fnet_b[j], xt, layout_out, mod3, nb, i, tm=f_tile, a_tile=stored_tile)
        xt = _moe(xt, layout_out, layout_out, mod3, nb, i, norm_gain[i, 1], router_w[i], router_b[i],
                  exp_w1, exp_b1, w2_bf16, exp_b2, final_gain if i == depth - 1 else None)
    return xt[:bsz * s_len].reshape(bsz, s_len, d)
```

```python
import functools

import jax
import jax.numpy as jnp
from jax import lax
from jax.experimental import pallas as pl
from jax.experimental.pallas import tpu as pltpu

F32 = jnp.float32
BF16 = jnp.bfloat16
I32 = jnp.int32

GRID_W = 64
N_MIXERS = 3
N_MOD = 6
RMS_EPS = 1e-6
POOL_WINDOWS = (2, 4, 8, 16)
POOL_HALO = 8
NA_HEAD_DIM = 32
NA_ROWS = 8
NA_COLS = 16
NA_KEY_CHUNK = 256
FNET_GROUPS = 4
TOP_K = 4
SWIGLU_LIMIT = 7.0
SWIGLU_ALPHA = 1.702
MOE_BLOCK = 512
LOG2_E = 1.4426950408889634
MASK_VALUE = -1e30
SUBLANES = 8
DMA_PRIORITIES = 2

LANES = 128
HEADS_PER_LANE_GROUP = LANES // NA_HEAD_DIM
MIB = 1024 * 1024


def _params(semantics, vmem_mib):
    return pltpu.CompilerParams(dimension_semantics=semantics, vmem_limit_bytes=vmem_mib * MIB)


def _norm_mod(x, gain, shift, scale):
    ms = jnp.mean(x * x, axis=-1, keepdims=True)
    y = x * lax.rsqrt(ms + RMS_EPS) * gain
    return y * (1.0 + scale) + shift


def _nt_dot(a, b):
    return lax.dot_general(a, b, (((1,), (1,)), ((), ())), preferred_element_type=F32)


def _mod_kernel(c_ref, w_ref, b_ref, o_ref):
    c = c_ref[...]
    s = (c * jax.nn.sigmoid(c)).astype(BF16)
    o_ref[0] = jnp.dot(s, w_ref[0].astype(BF16), preferred_element_type=F32) + b_ref[0]


def _modulation(c_all, w_mod, b_mod):
    depth, d, n = w_mod.shape
    nb = c_all.shape[0]
    tn = 1024
    return pl.pallas_call(
        _mod_kernel,
        grid=(depth, n // tn),
        in_specs=[pl.BlockSpec((nb, d), lambda i, j: (0, 0)),
                  pl.BlockSpec((1, d, tn), lambda i, j: (i, 0, j)),
                  pl.BlockSpec((1, 1, tn), lambda i, j: (i, 0, j))],
        out_specs=pl.BlockSpec((1, nb, tn), lambda i, j: (i, 0, j)),
        out_shape=jax.ShapeDtypeStruct((depth, nb, n), F32),
        compiler_params=_params(("arbitrary", "arbitrary"), 40),
        name="modulation",
    )(c_all, w_mod, b_mod.reshape(depth, 1, n))


class _Layout:
    def __init__(self, bsz, s_len, l_len, with_ctx):
        self.bsz, self.s_len, self.l_len, self.with_ctx = bsz, s_len, l_len, with_ctx
        self.n_lat = bsz * s_len
        self.n_tok = self.n_lat + (bsz * l_len if with_ctx else 0)

    def tile(self, want, seq_local=False):
        tm = min(want, self.s_len)
        if self.with_ctx:
            ctx_span = self.l_len if seq_local else self.bsz * self.l_len
            tm = min(tm, ctx_span)
            assert ctx_span % tm == 0
        assert self.s_len % tm == 0
        return tm

    def mod_row(self, t, tm):
        lat = (t * tm) // self.s_len
        if not self.with_ctx:
            return lat
        return jnp.where(t < self.n_lat // tm, lat, self.bsz)


def _mod_spec(mod3, layer, which, nb, layout, tm):
    base = (layer * N_MOD + which) * nb
    d = mod3.shape[-1]
    return pl.BlockSpec((1, 1, d), lambda t, *_: (base + layout.mod_row(t, tm), 0, 0))


def _pool_kernel(*refs, tm, s_len, l_len, n_lat_tiles, with_ctx, split_src):
    n_src = 6 if split_src else 3
    src_refs, (gain_ref, sh_ref, sc_ref, gt_ref, pw_ref, ps_ref, o_ref, ext_ref) = refs[:n_src], refs[n_src:]
    t = pl.program_id(0)
    if with_ctx:
        is_lat = t < n_lat_tiles
        tiles_per_seq = jnp.where(is_lat, s_len // tm, l_len // tm)
        tile_in_seq = jnp.where(is_lat, t % (s_len // tm), (t - n_lat_tiles) % (l_len // tm))
        seq_len = jnp.where(is_lat, s_len, l_len)
    else:
        tiles_per_seq = s_len // tm
        tile_in_seq = t % (s_len // tm)
        seq_len = s_len
    if split_src:
        xc, xp, xn = (jnp.where(is_lat, a[...], b[...]) for a, b in zip(src_refs[:3], src_refs[3:]))
    else:
        xc, xp, xn = (a[...] for a in src_refs)
    gain, shift, scale = gain_ref[...], sh_ref[0], sc_ref[0]
    hc = _norm_mod(xc, gain, shift, scale)
    hp = _norm_mod(xp, gain, shift, scale)
    hn = _norm_mod(xn, gain, shift, scale)
    ext_ref[0:POOL_HALO] = jnp.where(tile_in_seq > 0, hp, 0.0)
    ext_ref[POOL_HALO:POOL_HALO + tm] = hc
    ext_ref[POOL_HALO + tm:2 * POOL_HALO + tm] = jnp.where(tile_in_seq < tiles_per_seq - 1, hn, 0.0)
    pos = tile_in_seq * tm + lax.broadcasted_iota(I32, (tm, 1), 0)
    dg = hc.shape[1] // len(POOL_WINDOWS)
    gate, pscale = gt_ref[0], ps_ref[...]
    for g, w in enumerate(POOL_WINDOWS):
        cs = slice(g * dg, (g + 1) * dg)
        start = POOL_HALO - w // 2
        acc = ext_ref[start:start + tm, cs]
        for j in range(1, w):
            acc = acc + ext_ref[start + j:start + j + tm, cs]
        cnt = jnp.minimum(pos + w // 2, seq_len) - jnp.maximum(pos - w // 2, 0)
        diff = (acc / cnt.astype(F32) - hc[:, cs]).astype(BF16)
        y = jnp.dot(diff, pw_ref[g], preferred_element_type=F32)
        o_ref[:, cs] = xc[:, cs] + gate[:, cs] * (y * pscale[:, cs])


def _pool_mixer(x_lat, x_ctx, layout, mod3, nb, layer, gain, pool_w, pool_scale):
    d = x_lat.shape[1]
    tm = layout.tile(256, seq_local=True)
    hb = tm // POOL_HALO
    n_lat_tiles = layout.n_lat // tm
    split_src = x_ctx is not None

    def src_specs(n_rows, first_tile):
        def local(t):
            return jnp.clip(t - first_tile, 0, n_rows // tm - 1)
        return [pl.BlockSpec((tm, d), lambda t: (local(t), 0)),
                pl.BlockSpec((POOL_HALO, d), lambda t: (jnp.maximum(local(t) * hb - 1, 0), 0)),
                pl.BlockSpec((POOL_HALO, d), lambda t: (jnp.minimum((local(t) + 1) * hb, n_rows // POOL_HALO - 1), 0))]

    specs = src_specs(x_lat.shape[0], 0)
    srcs = [x_lat] * 3
    if split_src:
        specs += src_specs(x_ctx.shape[0], n_lat_tiles)
        srcs += [x_ctx] * 3
    kern = functools.partial(_pool_kernel, tm=tm, s_len=layout.s_len, l_len=layout.l_len,
                             n_lat_tiles=n_lat_tiles, with_ctx=layout.with_ctx, split_src=split_src)
    return pl.pallas_call(
        kern,
        grid=(layout.n_tok // tm,),
        in_specs=specs + [
                  pl.BlockSpec((1, d), lambda t: (0, 0)),
                  _mod_spec(mod3, layer, 0, nb, layout, tm),
                  _mod_spec(mod3, layer, 1, nb, layout, tm),
                  _mod_spec(mod3, layer, 2, nb, layout, tm),
                  pl.BlockSpec(pool_w.shape, lambda t: (0, 0, 0)),
                  pl.BlockSpec((1, d), lambda t: (0, 0))],
        out_specs=pl.BlockSpec((tm, d), lambda t: (t, 0)),
        out_shape=jax.ShapeDtypeStruct((layout.n_tok, d), F32),
        scratch_shapes=[pltpu.VMEM((tm + 2 * POOL_HALO, d), F32)],
        compiler_params=_params(("arbitrary",), 40),
        name="pool_mixer",
    )(*srcs, gain.reshape(1, d), mod3, mod3, mod3, pool_w.astype(BF16), pool_scale.reshape(1, d))


def _router_kernel(x_ref, gain_ref, sh_ref, sc_ref, wh_ref, wl_ref, rb_ref, h_ref, idx_ref, gate_ref):
    h = _norm_mod(x_ref[...], gain_ref[...], sh_ref[0], sc_ref[0])
    h_ref[...] = h
    h_hi = h.astype(BF16)
    h_lo = (h - h_hi.astype(F32)).astype(BF16)
    wh, wl = wh_ref[...], wl_ref[...]
    logits = _nt_dot(wh, h_hi) + _nt_dot(wh, h_lo) + _nt_dot(wl, h_hi) + rb_ref[...]
    n_exp = logits.shape[0]
    e_iota = lax.broadcasted_iota(I32, logits.shape, 0)
    work = logits
    vals, idxs = [], []
    for _ in range(TOP_K):
        m = jnp.max(work, axis=0, keepdims=True)
        idx = jnp.min(jnp.where(work == m, e_iota, n_exp), axis=0, keepdims=True)
        work = jnp.where(e_iota == idx, -jnp.inf, work)
        vals.append(m)
        idxs.append(idx)
    exps = [jnp.exp(v - vals[0]) for v in vals]
    denom = exps[0] + exps[1] + exps[2] + exps[3]
    pad_rows = idx_ref.shape[0] - TOP_K
    idx_ref[...] = jnp.concatenate(idxs + [jnp.zeros((pad_rows, logits.shape[1]), I32)], axis=0)
    gate_ref[...] = jnp.concatenate([e / denom for e in exps] + [jnp.zeros((pad_rows, logits.shape[1]), F32)],
                                    axis=0)


def _router(xt, layout, mod3, nb, layer, gain, router_w, router_b):
    n_tok, d = xt.shape
    n_exp = router_w.shape[1]
    tm = layout.tile(512)
    wt = router_w.T
    w_hi = wt.astype(BF16)
    w_lo = (wt - w_hi.astype(F32)).astype(BF16)
    return pl.pallas_call(
        _router_kernel,
        grid=(layout.n_tok // tm,),
        in_specs=[pl.BlockSpec((tm, d), lambda t: (t, 0)),
                  pl.BlockSpec((1, d), lambda t: (0, 0)),
                  _mod_spec(mod3, layer, 3, nb, layout, tm),
                  _mod_spec(mod3, layer, 4, nb, layout, tm),
                  pl.BlockSpec((n_exp, d), lambda t: (0, 0)),
                  pl.BlockSpec((n_exp, d), lambda t: (0, 0)),
                  pl.BlockSpec((n_exp, 1), lambda t: (0, 0))],
        out_specs=[pl.BlockSpec((tm, d), lambda t: (t, 0)),
                   pl.BlockSpec((8, tm), lambda t: (0, t)),
                   pl.BlockSpec((8, tm), lambda t: (0, t))],
        out_shape=[jax.ShapeDtypeStruct((layout.n_tok, d), F32),
                   jax.ShapeDtypeStruct((8, layout.n_tok), I32),
                   jax.ShapeDtypeStruct((8, layout.n_tok), F32)],
        compiler_params=_params(("arbitrary",), 40),
        name="moe_router",
    )(xt, gain.reshape(1, d), mod3, mod3, w_hi, w_lo, router_b.reshape(n_exp, 1))


def _rank_kernel(idx_ref, rank_ref, cnt_ref, run_ref, *, n_exp):
    t = pl.program_id(0)

    @pl.when(t == 0)
    def _():
        run_ref[...] = jnp.zeros_like(run_ref)

    tm = idx_ref.shape[1]
    e_iota = lax.broadcasted_iota(I32, (n_exp, tm), 0)
    sel = [e_iota == idx_ref[k:k + 1, :] for k in range(TOP_K)]
    member = sel[0] | sel[1] | sel[2] | sel[3]
    member_f = jnp.where(member, 1.0, 0.0)
    strictly_before = (lax.broadcasted_iota(I32, (tm, tm), 0) < lax.broadcasted_iota(I32, (tm, tm), 1))
    prefix = jnp.dot(member_f.astype(BF16), jnp.where(strictly_before, 1.0, 0.0).astype(BF16),
                     preferred_element_type=F32)
    base = run_ref[:, 0:1] + prefix
    ranks = [jnp.sum(jnp.where(s, base, 0.0), axis=0, keepdims=True).astype(I32) for s in sel]
    pad_rows = rank_ref.shape[0] - TOP_K
    rank_ref[...] = jnp.concatenate(ranks + [jnp.zeros((pad_rows, tm), I32)], axis=0)
    run_ref[...] = run_ref[...] + jnp.sum(member_f, axis=1, keepdims=True)
    cnt_ref[...] = run_ref[...].astype(I32)


def _ranks(idx_t, n_exp):
    n_tok = idx_t.shape[1]
    tm = 512 if n_tok % 512 == 0 else 256
    return pl.pallas_call(
        functools.partial(_rank_kernel, n_exp=n_exp),
        grid=(n_tok // tm,),
        in_specs=[pl.BlockSpec((8, tm), lambda t: (0, t))],
        out_specs=[pl.BlockSpec((8, tm), lambda t: (0, t)),
                   pl.BlockSpec((n_exp, LANES), lambda t: (0, 0))],
        out_shape=[jax.ShapeDtypeStruct((8, n_tok), I32),
                   jax.ShapeDtypeStruct((n_exp, LANES), I32)],
        scratch_shapes=[pltpu.VMEM((n_exp, LANES), F32)],
        compiler_params=_params(("arbitrary",), 32),
        name="moe_rank",
    )(idx_t)


def _dest_kernel(idx_ref, rank_ref, ps_ref, dest_ref, *, n_exp):
    tm = idx_ref.shape[1]
    e_iota = lax.broadcasted_iota(I32, (n_exp, tm), 0)
    ps = ps_ref[:, 0:1]
    rows = []
    for k in range(TOP_K):
        start = jnp.sum(jnp.where(e_iota == idx_ref[k:k + 1, :], ps, 0.0), axis=0, keepdims=True)
        rows.append(start.astype(I32) + rank_ref[k:k + 1, :])
    pad_rows = dest_ref.shape[0] - TOP_K
    dest_ref[...] = jnp.concatenate(rows + [jnp.zeros((pad_rows, tm), I32)], axis=0)


def _dests(idx_t, rank_t, pad_starts):
    n_tok = idx_t.shape[1]
    n_exp = pad_starts.shape[0]
    tm = 512 if n_tok % 512 == 0 else 256
    ps = jnp.broadcast_to(pad_starts.astype(F32)[:, None], (n_exp, LANES))
    return pl.pallas_call(
        functools.partial(_dest_kernel, n_exp=n_exp),
        grid=(n_tok // tm,),
        in_specs=[pl.BlockSpec((8, tm), lambda t: (0, t)),
                  pl.BlockSpec((8, tm), lambda t: (0, t)),
                  pl.BlockSpec((n_exp, LANES), lambda t: (0, 0))],
        out_specs=pl.BlockSpec((8, tm), lambda t: (0, t)),
        out_shape=jax.ShapeDtypeStruct((8, n_tok), I32),
        compiler_params=_params(("arbitrary",), 32),
        name="moe_dest",
    )(idx_t, rank_t, ps)


def _tile_major(dest_t, tm):
    n_tok = dest_t.shape[1]
    return dest_t[:TOP_K].reshape(TOP_K, n_tok // tm, tm).transpose(1, 0, 2).reshape(-1)


def _dispatch_kernel(cnt_ref, ps_ref, h_ref, dest_ref, xs_ref, zero_ref, sem, zsem, *, n_exp, block):
    t = pl.program_id(0)
    tm = h_ref.shape[0]

    def row_copy(r, d):
        return pltpu.make_async_copy(h_ref.at[pl.ds(r, 1)], xs_ref.at[pl.ds(d, 1)], sem)

    def issue(r8, carry):
        for u in range(SUBLANES):
            r = r8 * SUBLANES + u
            for k in range(TOP_K):
                row_copy(r, dest_ref[k * tm + r]).start(priority=k % DMA_PRIORITIES)
        return carry

    lax.fori_loop(0, tm // SUBLANES, issue, 0)

    @pl.when(t == 0)
    def _():
        zero_ref[...] = jnp.zeros_like(zero_ref)

        def zero_copy(d):
            return pltpu.make_async_copy(zero_ref.at[pl.ds(0, 1)], xs_ref.at[pl.ds(d, 1)], zsem)

        for e in range(n_exp):
            cnt = cnt_ref[e]
            first = ps_ref[e] + cnt
            n_pad = (-cnt) % block

            def zissue(i, carry, first=first):
                zero_copy(first + i).start()
                return carry

            def zwait(i, carry):
                zero_copy(0).wait()
                return carry

            lax.fori_loop(0, n_pad, zissue, 0)
            lax.fori_loop(0, n_pad, zwait, 0)

        def tail_copy(j):
            return pltpu.make_async_copy(zero_ref, xs_ref.at[pl.ds(j * block, block)], zsem)

        def tail_issue(j, carry):
            tail_copy(j).start()
            return carry

        def tail_wait(j, carry):
            tail_copy(j).wait()
            return carry

        first_tail = ps_ref[n_exp] // block
        lax.fori_loop(first_tail, xs_ref.shape[0] // block, tail_issue, 0)
        lax.fori_loop(first_tail, xs_ref.shape[0] // block, tail_wait, 0)

    for _ in range(TOP_K):
        pltpu.make_async_copy(zero_ref.at[pl.ds(0, tm)], xs_ref.at[pl.ds(0, tm)], sem).wait()


def _dispatch(h, dest_flat, counts, pad_starts, n_slots, tm):
    n_tok, d = h.shape
    n_exp = counts.shape[0]
    grid_spec = pltpu.PrefetchScalarGridSpec(
        num_scalar_prefetch=2,
        grid=(n_tok // tm,),
        in_specs=[pl.BlockSpec((tm, d), lambda t, *_: (t, 0)),
                  pl.BlockSpec((TOP_K * tm,), lambda t, *_: (t,), memory_space=pltpu.SMEM)],
        out_specs=pl.BlockSpec(memory_space=pl.ANY),
        scratch_shapes=[pltpu.VMEM((MOE_BLOCK, d), F32), pltpu.SemaphoreType.DMA, pltpu.SemaphoreType.DMA],
    )
    return pl.pallas_call(
        functools.partial(_dispatch_kernel, n_exp=n_exp, block=MOE_BLOCK),
        grid_spec=grid_spec,
        out_shape=jax.ShapeDtypeStruct((n_slots, d), F32),
        compiler_params=_params(("arbitrary",), 32),
        name="moe_dispatch",
    )(counts, pad_starts, h, dest_flat)


def _expert_kernel(be_ref, nu_ref, nv_ref, x_ref, w1_ref, b1_ref, w2_ref, b2_ref, y_ref):
    n_valid = nv_ref[pl.program_id(0)]
    half = x_ref.shape[0] // 2

    def ffn(x):
        f = w2_ref.shape[2]
        hg = jnp.dot(x.astype(BF16), w1_ref[0, 0].astype(BF16), preferred_element_type=F32) + b1_ref[0, 0]
        gate = jnp.minimum(hg[:, :f], SWIGLU_LIMIT)
        up = jnp.clip(hg[:, f:], -SWIGLU_LIMIT, SWIGLU_LIMIT)
        glu = gate * jax.nn.sigmoid(SWIGLU_ALPHA * gate)
        act = (glu * (up + 1.0)).astype(BF16)
        return jnp.dot(act, w2_ref[0, 0].astype(BF16), preferred_element_type=F32) + b2_ref[0, 0]

    @pl.when(n_valid > half)
    def _():
        y_ref[...] = ffn(x_ref[...])

    @pl.when((n_valid > 0) & (n_valid <= half))
    def _():
        y_ref[0:half] = ffn(x_ref[0:half])
        y_ref[half:] = jnp.zeros((x_ref.shape[0] - half, y_ref.shape[1]), y_ref.dtype)

    @pl.when(n_valid == 0)
    def _():
        y_ref[...] = jnp.zeros_like(y_ref)


def _experts(xs, block_expert, n_used, n_valid, w1, b1, w2, b2, layer):
    n_slots, d = xs.shape
    depth, n_exp, _, f2 = w1.shape
    f = w2.shape[2]
    bm = MOE_BLOCK

    def row_block(j, be, nu, nv):
        return (jnp.minimum(j, nu[0] - 1), 0)

    def expert_block(j, be, nu, nv):
        return (layer, be[j], 0, 0)

    grid_spec = pltpu.PrefetchScalarGridSpec(
        num_scalar_prefetch=3,
        grid=(n_slots // bm,),
        in_specs=[pl.BlockSpec((bm, d), row_block),
                  pl.BlockSpec((1, 1, d, f2), expert_block),
                  pl.BlockSpec((1, 1, 1, f2), expert_block),
                  pl.BlockSpec((1, 1, f, d), expert_block),
                  pl.BlockSpec((1, 1, 1, d), expert_block)],
        out_specs=pl.BlockSpec((bm, d), lambda j, be, nu, nv: (j, 0)),
    )
    return pl.pallas_call(
        _expert_kernel,
        grid_spec=grid_spec,
        out_shape=jax.ShapeDtypeStruct((n_slots, d), F32),
        compiler_params=_params(("arbitrary",), 62),
        name="moe_experts",
    )(block_expert, n_used, n_valid, xs, w1, b1.reshape(depth, n_exp, 1, f2), w2, b2.reshape(depth, n_exp, 1, d))


def _combine_kernel(x_ref, g_ref, dest_ref, gt_ref, fg_ref, ys_ref, o_ref, buf_ref, sem, *, final_norm):
    tm = x_ref.shape[0]

    def row_copy(k, r8, u, d):
        return pltpu.make_async_copy(ys_ref.at[pl.ds(d, 1)], buf_ref.at[k, r8, pl.ds(u, 1)], sem)

    def issue(r8, carry):
        for u in range(SUBLANES):
            for k in range(TOP_K):
                row_copy(k, r8, u, dest_ref[k * tm + r8 * SUBLANES + u]).start(priority=k % DMA_PRIORITIES)
        return carry

    lax.fori_loop(0, tm // SUBLANES, issue, 0)

    g8 = g_ref[...]
    g_cols = jnp.concatenate([g8, jnp.zeros((tm - 8, tm), F32)], axis=0).T

    for k in range(TOP_K):
        pltpu.make_async_copy(ys_ref.at[pl.ds(0, tm)], o_ref, sem).wait()

    d = x_ref.shape[1]
    acc = g_cols[:, 0:1] * buf_ref[0].reshape(tm, d)
    for k in range(1, TOP_K):
        acc = acc + g_cols[:, k:k + 1] * buf_ref[k].reshape(tm, d)
    out = x_ref[...] + gt_ref[0] * acc
    if final_norm:
        ms = jnp.mean(out * out, axis=-1, keepdims=True)
        out = out * lax.rsqrt(ms + RMS_EPS) * fg_ref[...]
    o_ref[...] = out


def _combine(xt, out_layout, gates_t, dest_flat, ys, mod3, nb, layer, final_gain, tm):
    d = xt.shape[1]
    final_norm = final_gain is not None
    fg = (final_gain if final_norm else jnp.ones((d,), F32)).reshape(1, d)
    return pl.pallas_call(
        functools.partial(_combine_kernel, final_norm=final_norm),
        grid=(out_layout.n_tok // tm,),
        in_specs=[pl.BlockSpec((tm, d), lambda t: (t, 0)),
                  pl.BlockSpec((8, tm), lambda t: (0, t)),
                  pl.BlockSpec((TOP_K * tm,), lambda t: (t,), memory_space=pltpu.SMEM),
                  _mod_spec(mod3, layer, 5, nb, out_layout, tm),
                  pl.BlockSpec((1, d), lambda t: (0, 0)),
                  pl.BlockSpec(memory_space=pl.ANY)],
        out_specs=pl.BlockSpec((tm, d), lambda t: (t, 0)),
        out_shape=jax.ShapeDtypeStruct((out_layout.n_tok, d), F32),
        scratch_shapes=[pltpu.VMEM((TOP_K, tm // SUBLANES, SUBLANES, d), F32), pltpu.SemaphoreType.DMA],
        compiler_params=_params(("arbitrary",), 32),
        name="moe_combine",
    )(xt, gates_t, dest_flat, mod3, fg, ys)


def _moe(xt, layout, out_layout, mod3, nb, layer, gain, router_w, router_b, w1, b1, w2, b2, final_gain):
    n_exp = router_w.shape[1]
    h, idx_t, gates_t = _router(xt, layout, mod3, nb, layer, gain, router_w, router_b)
    rank_t, cnt = _ranks(idx_t, n_exp)
    counts = cnt[:, 0]
    padded = (counts + MOE_BLOCK - 1) // MOE_BLOCK * MOE_BLOCK
    pad_ends = jnp.cumsum(padded)
    pad_starts = pad_ends - padded
    n_blocks = -(-layout.n_tok * TOP_K // MOE_BLOCK) + n_exp
    n_slots = n_blocks * MOE_BLOCK
    block_first_row = jnp.arange(n_blocks, dtype=I32) * MOE_BLOCK
    block_expert = jnp.minimum(jnp.sum(pad_ends[None, :] <= block_first_row[:, None], axis=1), n_exp - 1).astype(I32)
    n_used = (pad_ends[-1:] // MOE_BLOCK).astype(I32)
    dest_t = _dests(idx_t, rank_t, pad_starts)
    tm_d = layout.tile(256)
    tm_c = tm_d
    slot_bounds = jnp.concatenate([pad_starts, pad_ends[-1:]]).astype(I32)
    xs = _dispatch(h, _tile_major(dest_t, tm_d), counts, slot_bounds, n_slots, tm_d)
    rows_before = block_first_row - pad_starts[block_expert]
    n_valid = jnp.clip(counts[block_expert] - rows_before, 0, MOE_BLOCK).astype(I32)
    ys = _experts(xs, block_expert, n_used, n_valid, w1, b1, w2, b2, layer)
    return _combine(xt, out_layout, gates_t, _tile_major(dest_t, tm_c), ys, mod3, nb, layer, final_gain, tm_c)


def _nm_matmul_kernel(x_ref, gain_ref, sh_ref, sc_ref, w_ref, b_ref, o_ref, h_ref):
    @pl.when(pl.program_id(1) == 0)
    def _():
        h_ref[...] = _norm_mod(x_ref[...], gain_ref[...], sh_ref[0], sc_ref[0]).astype(BF16)

    o_ref[...] = (jnp.dot(h_ref[...], w_ref[...], preferred_element_type=F32) + b_ref[...]).astype(o_ref.dtype)


def _nm_matmul(xt, layout, mod3, nb, layer, gain, w, b, tm_want, tn):
    d = xt.shape[1]
    n = w.shape[1]
    tm = layout.tile(tm_want)
    return pl.pallas_call(
        _nm_matmul_kernel,
        grid=(layout.n_tok // tm, n // tn),
        in_specs=[pl.BlockSpec((tm, d), lambda m, j: (m, 0)),
                  pl.BlockSpec((1, d), lambda m, j: (0, 0)),
                  _mod_spec(mod3, layer, 0, nb, layout, tm),
                  _mod_spec(mod3, layer, 1, nb, layout, tm),
                  pl.BlockSpec((d, tn), lambda m, j: (0, j)),
                  pl.BlockSpec((1, tn), lambda m, j: (0, j))],
        out_specs=pl.BlockSpec((tm, tn), lambda m, j: (m, j)),
        out_shape=jax.ShapeDtypeStruct((layout.n_tok, n), BF16),
        scratch_shapes=[pltpu.VMEM((tm, d), BF16)],
        compiler_params=_params(("arbitrary", "arbitrary"), 48),
        name="norm_mod_matmul",
    )(xt, gain.reshape(1, d), mod3, mod3, w.astype(BF16), b.reshape(1, n))


def _proj_residual_kernel(a_ref, w_ref, b_ref, x_ref, gt_ref, o_ref):
    y = jnp.dot(a_ref[...], w_ref[...], preferred_element_type=F32) + b_ref[...]
    o_ref[...] = x_ref[...] + gt_ref[0] * y


def _proj_residual(a, w, b, xt, out_layout, mod3, nb, layer, tm=None, a_tile=lambda m: m):
    k, n = w.shape
    tm = out_layout.tile(512) if tm is None else tm
    assert out_layout.s_len % tm == 0
    return pl.pallas_call(
        _proj_residual_kernel,
        grid=(out_layout.n_tok // tm,),
        in_specs=[pl.BlockSpec((tm, k), lambda m: (a_tile(m), 0)),
                  pl.BlockSpec((k, n), lambda m: (0, 0)),
                  pl.BlockSpec((1, n), lambda m: (0, 0)),
                  pl.BlockSpec((tm, n), lambda m: (m, 0)),
                  _mod_spec(mod3, layer, 2, nb, out_layout, tm)],
        out_specs=pl.BlockSpec((tm, n), lambda m: (m, 0)),
        out_shape=jax.ShapeDtypeStruct((out_layout.n_tok, n), F32),
        compiler_params=_params(("arbitrary",), 48),
        name="proj_residual",
    )(a, w.astype(BF16), b.reshape(1, n), xt, mod3)


def _na_bias_slabs(rpb, rows):
    n_heads = rpb.shape[0]
    kr = min(NA_ROWS, rows)
    col = jnp.arange(GRID_W)
    c0 = jnp.clip(col - NA_COLS // 2, 0, GRID_W - NA_COLS)
    col_mask = (col[None, :] >= c0[:, None]) & (col[None, :] < c0[:, None] + NA_COLS)
    dc_idx = jnp.clip(col[None, :] - col[:, None] + NA_COLS - 1, 0, 2 * NA_COLS - 2)
    n_place = NA_ROWS
    dr = jnp.arange(n_place)[:, None] + jnp.arange(kr)[None, :]
    bias = rpb[:, dr][:, :, :, dc_idx]
    bias = jnp.where(col_mask[None, None, None], bias * LOG2_E, MASK_VALUE)
    bias = jnp.transpose(bias, (1, 0, 3, 2, 4))
    return bias.reshape(n_place, n_heads // HEADS_PER_LANE_GROUP, HEADS_PER_LANE_GROUP * GRID_W, kr * GRID_W)


def _na_kernel(q_ref, k_ref, v_ref, kc_ref, vc_ref, bias_ref, o_ref, *, rows, kr, n_groups):
    r = pl.program_id(2)
    r0 = jnp.clip(r - kr // 2, 0, rows - kr)
    base = pl.multiple_of(r0 * GRID_W, GRID_W)
    n_lat, n_ctx = kr * GRID_W, kc_ref.shape[0]
    ch = NA_KEY_CHUNK
    nq = HEADS_PER_LANE_GROUP * GRID_W
    head_of_row = lax.broadcasted_iota(I32, (nq, LANES), 0) // GRID_W
    head_of_lane = lax.broadcasted_iota(I32, (nq, LANES), 1) // NA_HEAD_DIM
    own = head_of_row == head_of_lane
    for g in range(n_groups):
        gs = slice(g * LANES, (g + 1) * LANES)
        qg = q_ref[:, gs]
        qbd = jnp.where(own, jnp.concatenate([qg] * HEADS_PER_LANE_GROUP, axis=0), jnp.zeros((), BF16))
        m = l = acc = None
        for c in range((n_lat + n_ctx) // ch):
            if c < n_lat // ch:
                rows_c = pl.ds(base + c * ch, ch)
                s = _nt_dot(qbd, k_ref[rows_c, gs]) + bias_ref[0, g, :, c * ch:(c + 1) * ch]
                vals = v_ref[rows_c, gs]
            else:
                rows_c = slice(c * ch - n_lat, (c + 1) * ch - n_lat)
                s = _nt_dot(qbd, kc_ref[rows_c, gs])
                vals = vc_ref[rows_c, gs]
            m_c = jnp.max(s, axis=-1, keepdims=True)
            if m is None:
                m = m_c
                p = jnp.exp2(s - m)
                l = jnp.sum(p, axis=-1, keepdims=True)
                acc = jnp.dot(p.astype(BF16), vals, preferred_element_type=F32)
            else:
                m_new = jnp.maximum(m, m_c)
                alpha = jnp.exp2(m - m_new)
                p = jnp.exp2(s - m_new)
                l = l * alpha + jnp.sum(p, axis=-1, keepdims=True)
                acc = acc * alpha + jnp.dot(p.astype(BF16), vals, preferred_element_type=F32)
                m = m_new
        o = jnp.where(own, acc / l, 0.0)
        og = o[0:GRID_W]
        for h in range(1, HEADS_PER_LANE_GROUP):
            og = og + o[h * GRID_W:(h + 1) * GRID_W]
        o_ref[:, gs] = og.astype(o_ref.dtype)


def _neighbourhood_attention(qkv, bias_slabs, bsz, s_len, l_len, d):
    rows = s_len // GRID_W
    kr = min(NA_ROWS, rows)
    n_split = 1
    dh = d // n_split
    n_groups = dh // LANES
    ctx_block0 = bsz * s_len // l_len

    def place(r):
        r0 = jnp.clip(r - kr // 2, 0, rows - kr)
        return r0 - r + NA_ROWS - 1

    return pl.pallas_call(
        functools.partial(_na_kernel, rows=rows, kr=kr, n_groups=n_groups),
        grid=(bsz, n_split, rows),
        in_specs=[pl.BlockSpec((GRID_W, dh), lambda b, h, r: (b * rows + r, h)),
                  pl.BlockSpec((s_len, dh), lambda b, h, r: (b, n_split + h)),
                  pl.BlockSpec((s_len, dh), lambda b, h, r: (b, 2 * n_split + h)),
                  pl.BlockSpec((l_len, dh), lambda b, h, r: (ctx_block0 + b, n_split + h)),
                  pl.BlockSpec((l_len, dh), lambda b, h, r: (ctx_block0 + b, 2 * n_split + h)),
                  pl.BlockSpec((1, n_groups, HEADS_PER_LANE_GROUP * GRID_W, kr * GRID_W),
                               lambda b, h, r: (place(r), h, 0, 0))],
        out_specs=pl.BlockSpec((GRID_W, dh), lambda b, h, r: (b * rows + r, h)),
        out_shape=jax.ShapeDtypeStruct((bsz * s_len, d), BF16),
        compiler_params=_params(("arbitrary", "arbitrary", "arbitrary"), 60),
        name="neighbourhood_attention",
    )(qkv, qkv, qkv, qkv, qkv, bias_slabs)


def _dft_cos_sin(n, rows=None):
    j = jnp.arange(n, dtype=I32)
    k = j if rows is None else rows.astype(I32)
    ang = ((k[..., None] * j) % n).astype(F32) * (2.0 * jnp.pi / n)
    norm = n ** -0.5
    return jnp.cos(ang) * norm, jnp.sin(ang) * norm


def _fnet_stage1_kernel(x_ref, gain_ref, sh_ref, sc_ref, cs_ref, o_ref):
    h = _norm_mod(x_ref[...], gain_ref[...], sh_ref[0], sc_ref[0]).astype(BF16)
    dg = cs_ref.shape[0]
    for g in range(FNET_GROUPS):
        cols = slice(g * dg, (g + 1) * dg)
        a = jnp.dot(h[:, cols], cs_ref[...], preferred_element_type=F32)
        o_ref[0, 0, :, cols] = a[:, :dg].astype(o_ref.dtype)
        o_ref[0, 1, :, cols] = a[:, dg:].astype(o_ref.dtype)


def _fnet_stage1(xt, layout, mod3, nb, layer, gain):
    d = xt.shape[1]
    dg = d // FNET_GROUPS
    tm = layout.tile(512)
    tiles_per_seq = layout.s_len // tm
    cc, sc = _dft_cos_sin(dg)
    cs = jnp.concatenate([cc, sc], axis=1).astype(BF16)
    return pl.pallas_call(
        _fnet_stage1_kernel,
        grid=(layout.n_lat // tm,),
        in_specs=[pl.BlockSpec((tm, d), lambda t: (t, 0)),
                  pl.BlockSpec((1, d), lambda t: (0, 0)),
                  _mod_spec(mod3, layer, 0, nb, layout, tm),
                  _mod_spec(mod3, layer, 1, nb, layout, tm),
                  pl.BlockSpec((dg, 2 * dg), lambda t: (0, 0))],
        out_specs=pl.BlockSpec((1, 2, tm, d), lambda t: (t // tiles_per_seq, 0, t % tiles_per_seq, 0)),
        out_shape=jax.ShapeDtypeStruct((layout.bsz, 2, layout.s_len, d), BF16),
        compiler_params=_params(("arbitrary",), 40),
        name="fnet_channel_dft",
    )(xt, gain.reshape(1, d), mod3, mod3, cs)


def _fnet_stage2_kernel(lc_ref, ls_ref, rev_ref, ac_ref, as_ref, o_ref):
    tm = o_ref.shape[2]
    p = jnp.dot(lc_ref[0], ac_ref[0, 0], preferred_element_type=F32)
    q = jnp.dot(ls_ref[0], as_ref[0, 0], preferred_element_type=F32)
    o_ref[0, 0] = (p - q)[:tm].astype(o_ref.dtype)
    z = (p + q).astype(BF16)
    o_ref[0, 1] = jnp.dot(rev_ref[...], z, preferred_element_type=F32).astype(o_ref.dtype)


def _fnet_stage2(a, s_len):
    bsz, _, _, d = a.shape
    half = s_len // 2
    tm = min(512, half)
    tn = min(1024, d)
    m_tiles = half // tm
    ext = tm + SUBLANES
    rows = (jnp.arange(m_tiles)[:, None] * tm + jnp.arange(ext)[None, :]) % s_len
    lhs_cos, lhs_sin = (t.astype(BF16) for t in _dft_cos_sin(s_len, rows))
    rev = (jnp.arange(ext)[None, :] == tm - jnp.arange(tm)[:, None]).astype(BF16)
    out = pl.pallas_call(
        _fnet_stage2_kernel,
        grid=(bsz, d // tn, m_tiles),
        in_specs=[pl.BlockSpec((1, ext, s_len), lambda b, j, m: (m, 0, 0)),
                  pl.BlockSpec((1, ext, s_len), lambda b, j, m: (m, 0, 0)),
                  pl.BlockSpec((tm, ext), lambda b, j, m: (0, 0)),
                  pl.BlockSpec((1, 1, s_len, tn), lambda b, j, m: (b, 0, 0, j)),
                  pl.BlockSpec((1, 1, s_len, tn), lambda b, j, m: (b, 1, 0, j))],
        out_specs=pl.BlockSpec((1, 2, tm, tn), lambda b, j, m: (b, 0, m, j)),
        out_shape=jax.ShapeDtypeStruct((bsz, 2, half, d), BF16),
        compiler_params=_params(("arbitrary", "arbitrary", "arbitrary"), 48),
        name="fnet_position_dft",
    )(lhs_cos, lhs_sin, rev, a, a)

    def stored_tile(t):
        b, w = t // (2 * m_tiles), t % (2 * m_tiles)
        return b * 2 * m_tiles + jnp.where(w < m_tiles, w, 3 * m_tiles - 1 - w)

    return out.reshape(bsz * s_len, d), tm, stored_tile


def kernel(x, c, ctx, c_ctx, w_mod, b_mod, norm_gain, final_gain, pool_w, pool_scale, na_w_qkv, na_b_qkv, na_rpb, na_w_o, na_b_o, fnet_w, fnet_b, router_w, router_b, exp_w1, exp_b1, exp_w2, exp_b2):
    bsz, s_len, d = x.shape
    l_len = ctx.shape[1]
    depth = w_mod.shape[0]
    last_ctx_layer = ((depth - 2) // N_MIXERS) * N_MIXERS + 1

    nb = -(-(bsz + 1) // 8) * 8
    c_all = jnp.concatenate([c, c_ctx[None], jnp.zeros((nb - bsz - 1, d), F32)], axis=0)
    mod = _modulation(c_all, w_mod, b_mod)
    mod3 = mod.reshape(depth, nb, N_MOD, d).transpose(0, 2, 1, 3).reshape(depth * N_MOD * nb, 1, d)

    lat_only = _Layout(bsz, s_len, l_len, False)
    with_ctx = _Layout(bsz, s_len, l_len, True)
    x_lat, x_ctx = x.reshape(bsz * s_len, d), ctx.reshape(bsz * l_len, d)
    xt = None
    for i in range(depth):
        kind, j = i % N_MIXERS, i // N_MIXERS
        ctx_in = i <= last_ctx_layer
        ctx_out = i < last_ctx_layer
        layout_in = with_ctx if ctx_in else lat_only
        layout_out = with_ctx if ctx_out else lat_only
        if kind == 0:
            if xt is None:
                srcs = (x_lat, x_ctx if ctx_out else None)
            else:
                srcs = (xt, None)
            xt = _pool_mixer(*srcs, layout_out, mod3, nb, i, norm_gain[i, 0], pool_w[j], pool_scale[j])
        elif kind == 1:
            if ctx_in:
                k_fold = jnp.concatenate([jnp.ones((d,), F32), jnp.full((d,), NA_HEAD_DIM ** -0.5 * LOG2_E, F32),
                                          jnp.ones((d,), F32)])
                qkv = _nm_matmul(xt, layout_in, mod3, nb, i, norm_gain[i, 0], na_w_qkv[j] * k_fold,
                                 na_b_qkv[j] * k_fold, 1024, min(1024, d))
                slabs = _na_bias_slabs(na_rpb[j], s_len // GRID_W)
                att = _neighbourhood_attention(qkv, slabs, bsz, s_len, l_len, d)
                xt = _proj_residual(att, na_w_o[j], na_b_o[j], xt, layout_out, mod3, nb, i)
            else:
                raise NotImplementedError("neighbourhood attention without a live context stream")
        else:
            a = _fnet_stage1(xt, layout_out, mod3, nb, i, norm_gain[i, 0])
            f, f_tile, stored_tile = _fnet_stage2(a, s_len)
            xt = _proj_residual(f, fnet_w[j], fnet_b[j], xt, layout_out, mod3, nb, i, tm=f_tile, a_tile=stored_tile)
        xt = _moe(xt, layout_out, layout_out, mod3, nb, i, norm_gain[i, 1], router_w[i], router_b[i],
                  exp_w1, exp_b1, exp_w2, exp_b2, final_gain if i == depth - 1 else None)
    return xt[:bsz * s_len].reshape(bsz, s_len, d)
```

```python
import functools

import jax
import jax.numpy as jnp
from jax import lax
from jax.experimental import pallas as pl
from jax.experimental.pallas import tpu as pltpu

F32 = jnp.float32
BF16 = jnp.bfloat16
I32 = jnp.int32

GRID_W = 64
N_MIXERS = 3
N_MOD = 6
RMS_EPS = 1e-6
POOL_WINDOWS = (2, 4, 8, 16)
POOL_HALO = 8
NA_HEAD_DIM = 32
NA_ROWS = 8
NA_COLS = 16
NA_KEY_CHUNK = 256
FNET_GROUPS = 4
TOP_K = 4
SWIGLU_LIMIT = 7.0
SWIGLU_ALPHA = 1.702
MOE_BLOCK = 512
LOG2_E = 1.4426950408889634
MASK_VALUE = -1e30
SUBLANES = 8
DMA_PRIORITIES = 2

LANES = 128
HEADS_PER_LANE_GROUP = LANES // NA_HEAD_DIM
MIB = 1024 * 1024


def _params(semantics, vmem_mib):
    return pltpu.CompilerParams(dimension_semantics=semantics, vmem_limit_bytes=vmem_mib * MIB)


def _norm_mod(x, gain, shift, scale):
    ms = jnp.mean(x * x, axis=-1, keepdims=True)
    y = x * lax.rsqrt(ms + RMS_EPS) * gain
    return y * (1.0 + scale) + shift


def _nt_dot(a, b):
    return lax.dot_general(a, b, (((1,), (1,)), ((), ())), preferred_element_type=F32)


def _mod_kernel(c_ref, w_ref, b_ref, o_ref):
    c = c_ref[...]
    s = (c * jax.nn.sigmoid(c)).astype(BF16)
    o_ref[0] = jnp.dot(s, w_ref[0].astype(BF16), preferred_element_type=F32) + b_ref[0]


def _modulation(c_all, w_mod, b_mod):
    depth, d, n = w_mod.shape
    nb = c_all.shape[0]
    tn = 1024
    return pl.pallas_call(
        _mod_kernel,
        grid=(depth, n // tn),
        in_specs=[pl.BlockSpec((nb, d), lambda i, j: (0, 0)),
                  pl.BlockSpec((1, d, tn), lambda i, j: (i, 0, j)),
                  pl.BlockSpec((1, 1, tn), lambda i, j: (i, 0, j))],
        out_specs=pl.BlockSpec((1, nb, tn), lambda i, j: (i, 0, j)),
        out_shape=jax.ShapeDtypeStruct((depth, nb, n), F32),
        compiler_params=_params(("arbitrary", "arbitrary"), 40),
        name="modulation",
    )(c_all, w_mod, b_mod.reshape(depth, 1, n))


class _Layout:
    def __init__(self, bsz, s_len, l_len, with_ctx):
        self.bsz, self.s_len, self.l_len, self.with_ctx = bsz, s_len, l_len, with_ctx
        self.n_lat = bsz * s_len
        self.n_tok = self.n_lat + (bsz * l_len if with_ctx else 0)

    def tile(self, want, seq_local=False):
        tm = min(want, self.s_len)
        if self.with_ctx:
            ctx_span = self.l_len if seq_local else self.bsz * self.l_len
            tm = min(tm, ctx_span)
            assert ctx_span % tm == 0
        assert self.s_len % tm == 0
        return tm

    def mod_row(self, t, tm):
        lat = (t * tm) // self.s_len
        if not self.with_ctx:
            return lat
        return jnp.where(t < self.n_lat // tm, lat, self.bsz)


def _mod_spec(mod3, layer, which, nb, layout, tm):
    base = (layer * N_MOD + which) * nb
    d = mod3.shape[-1]
    return pl.BlockSpec((1, 1, d), lambda t, *_: (base + layout.mod_row(t, tm), 0, 0))


def _pool_kernel(*refs, tm, s_len, l_len, n_lat_tiles, with_ctx, split_src):
    n_src = 6 if split_src else 3
    src_refs, (gain_ref, sh_ref, sc_ref, gt_ref, pw_ref, ps_ref, o_ref, ext_ref) = refs[:n_src], refs[n_src:]
    t = pl.program_id(0)
    if with_ctx:
        is_lat = t < n_lat_tiles
        tiles_per_seq = jnp.where(is_lat, s_len // tm, l_len // tm)
        tile_in_seq = jnp.where(is_lat, t % (s_len // tm), (t - n_lat_tiles) % (l_len // tm))
        seq_len = jnp.where(is_lat, s_len, l_len)
    else:
        tiles_per_seq = s_len // tm
        tile_in_seq = t % (s_len // tm)
        seq_len = s_len
    if split_src:
        xc, xp, xn = (jnp.where(is_lat, a[...], b[...]) for a, b in zip(src_refs[:3], src_refs[3:]))
    else:
        xc, xp, xn = (a[...] for a in src_refs)
    gain, shift, scale = gain_ref[...], sh_ref[0], sc_ref[0]
    hc = _norm_mod(xc, gain, shift, scale)
    hp = _norm_mod(xp, gain, shift, scale)
    hn = _norm_mod(xn, gain, shift, scale)
    ext_ref[0:POOL_HALO] = jnp.where(tile_in_seq > 0, hp, 0.0)
    ext_ref[POOL_HALO:POOL_HALO + tm] = hc
    ext_ref[POOL_HALO + tm:2 * POOL_HALO + tm] = jnp.where(tile_in_seq < tiles_per_seq - 1, hn, 0.0)
    pos = tile_in_seq * tm + lax.broadcasted_iota(I32, (tm, 1), 0)
    dg = hc.shape[1] // len(POOL_WINDOWS)
    gate, pscale = gt_ref[0], ps_ref[...]
    for g, w in enumerate(POOL_WINDOWS):
        cs = slice(g * dg, (g + 1) * dg)
        start = POOL_HALO - w // 2
        acc = ext_ref[start:start + tm, cs]
        for j in range(1, w):
            acc = acc + ext_ref[start + j:start + j + tm, cs]
        cnt = jnp.minimum(pos + w // 2, seq_len) - jnp.maximum(pos - w // 2, 0)
        diff = (acc / cnt.astype(F32) - hc[:, cs]).astype(BF16)
        y = jnp.dot(diff, pw_ref[g], preferred_element_type=F32)
        o_ref[:, cs] = xc[:, cs] + gate[:, cs] * (y * pscale[:, cs])


def _pool_mixer(x_lat, x_ctx, layout, mod3, nb, layer, gain, pool_w, pool_scale):
    d = x_lat.shape[1]
    tm = layout.tile(256, seq_local=True)
    hb = tm // POOL_HALO
    n_lat_tiles = layout.n_lat // tm
    split_src = x_ctx is not None

    def src_specs(n_rows, first_tile):
        def local(t):
            return jnp.clip(t - first_tile, 0, n_rows // tm - 1)
        return [pl.BlockSpec((tm, d), lambda t: (local(t), 0)),
                pl.BlockSpec((POOL_HALO, d), lambda t: (jnp.maximum(local(t) * hb - 1, 0), 0)),
                pl.BlockSpec((POOL_HALO, d), lambda t: (jnp.minimum((local(t) + 1) * hb, n_rows // POOL_HALO - 1), 0))]

    specs = src_specs(x_lat.shape[0], 0)
    srcs = [x_lat] * 3
    if split_src:
        specs += src_specs(x_ctx.shape[0], n_lat_tiles)
        srcs += [x_ctx] * 3
    kern = functools.partial(_pool_kernel, tm=tm, s_len=layout.s_len, l_len=layout.l_len,
                             n_lat_tiles=n_lat_tiles, with_ctx=layout.with_ctx, split_src=split_src)
    return pl.pallas_call(
        kern,
        grid=(layout.n_tok // tm,),
        in_specs=specs + [
                  pl.BlockSpec((1, d), lambda t: (0, 0)),
                  _mod_spec(mod3, layer, 0, nb, layout, tm),
                  _mod_spec(mod3, layer, 1, nb, layout, tm),
                  _mod_spec(mod3, layer, 2, nb, layout, tm),
                  pl.BlockSpec(pool_w.shape, lambda t: (0, 0, 0)),
                  pl.BlockSpec((1, d), lambda t: (0, 0))],
        out_specs=pl.BlockSpec((tm, d), lambda t: (t, 0)),
        out_shape=jax.ShapeDtypeStruct((layout.n_tok, d), F32),
        scratch_shapes=[pltpu.VMEM((tm + 2 * POOL_HALO, d), F32)],
        compiler_params=_params(("arbitrary",), 40),
        name="pool_mixer",
    )(*srcs, gain.reshape(1, d), mod3, mod3, mod3, pool_w.astype(BF16), pool_scale.reshape(1, d))


def _router_kernel(x_ref, gain_ref, sh_ref, sc_ref, wh_ref, wl_ref, rb_ref, h_ref, idx_ref, gate_ref):
    h = _norm_mod(x_ref[...], gain_ref[...], sh_ref[0], sc_ref[0])
    h_ref[...] = h
    h_hi = h.astype(BF16)
    h_lo = (h - h_hi.astype(F32)).astype(BF16)
    wh, wl = wh_ref[...], wl_ref[...]
    logits = _nt_dot(wh, h_hi) + _nt_dot(wh, h_lo) + _nt_dot(wl, h_hi) + rb_ref[...]
    n_exp = logits.shape[0]
    e_iota = lax.broadcasted_iota(I32, logits.shape, 0)
    work = logits
    vals, idxs = [], []
    for _ in range(TOP_K):
        m = jnp.max(work, axis=0, keepdims=True)
        idx = jnp.min(jnp.where(work == m, e_iota, n_exp), axis=0, keepdims=True)
        work = jnp.where(e_iota == idx, -jnp.inf, work)
        vals.append(m)
        idxs.append(idx)
    exps = [jnp.exp(v - vals[0]) for v in vals]
    denom = exps[0] + exps[1] + exps[2] + exps[3]
    pad_rows = idx_ref.shape[0] - TOP_K
    idx_ref[...] = jnp.concatenate(idxs + [jnp.zeros((pad_rows, logits.shape[1]), I32)], axis=0)
    gate_ref[...] = jnp.concatenate([e / denom for e in exps] + [jnp.zeros((pad_rows, logits.shape[1]), F32)],
                                    axis=0)


def _router(xt, layout, mod3, nb, layer, gain, router_w, router_b):
    n_tok, d = xt.shape
    n_exp = router_w.shape[1]
    tm = layout.tile(512)
    wt = router_w.T
    w_hi = wt.astype(BF16)
    w_lo = (wt - w_hi.astype(F32)).astype(BF16)
    return pl.pallas_call(
        _router_kernel,
        grid=(layout.n_tok // tm,),
        in_specs=[pl.BlockSpec((tm, d), lambda t: (t, 0)),
                  pl.BlockSpec((1, d), lambda t: (0, 0)),
                  _mod_spec(mod3, layer, 3, nb, layout, tm),
                  _mod_spec(mod3, layer, 4, nb, layout, tm),
                  pl.BlockSpec((n_exp, d), lambda t: (0, 0)),
                  pl.BlockSpec((n_exp, d), lambda t: (0, 0)),
                  pl.BlockSpec((n_exp, 1), lambda t: (0, 0))],
        out_specs=[pl.BlockSpec((tm, d), lambda t: (t, 0)),
                   pl.BlockSpec((8, tm), lambda t: (0, t)),
                   pl.BlockSpec((8, tm), lambda t: (0, t))],
        out_shape=[jax.ShapeDtypeStruct((layout.n_tok, d), F32),
                   jax.ShapeDtypeStruct((8, layout.n_tok), I32),
                   jax.ShapeDtypeStruct((8, layout.n_tok), F32)],
        compiler_params=_params(("arbitrary",), 40),
        name="moe_router",
    )(xt, gain.reshape(1, d), mod3, mod3, w_hi, w_lo, router_b.reshape(n_exp, 1))


def _rank_kernel(idx_ref, rank_ref, cnt_ref, run_ref, *, n_exp):
    t = pl.program_id(0)

    @pl.when(t == 0)
    def _():
        run_ref[...] = jnp.zeros_like(run_ref)

    tm = idx_ref.shape[1]
    e_iota = lax.broadcasted_iota(I32, (n_exp, tm), 0)
    sel = [e_iota == idx_ref[k:k + 1, :] for k in range(TOP_K)]
    member = sel[0] | sel[1] | sel[2] | sel[3]
    member_f = jnp.where(member, 1.0, 0.0)
    strictly_before = (lax.broadcasted_iota(I32, (tm, tm), 0) < lax.broadcasted_iota(I32, (tm, tm), 1))
    prefix = jnp.dot(member_f.astype(BF16), jnp.where(strictly_before, 1.0, 0.0).astype(BF16),
                     preferred_element_type=F32)
    base = run_ref[:, 0:1] + prefix
    ranks = [jnp.sum(jnp.where(s, base, 0.0), axis=0, keepdims=True).astype(I32) for s in sel]
    pad_rows = rank_ref.shape[0] - TOP_K
    rank_ref[...] = jnp.concatenate(ranks + [jnp.zeros((pad_rows, tm), I32)], axis=0)
    run_ref[...] = run_ref[...] + jnp.sum(member_f, axis=1, keepdims=True)
    cnt_ref[...] = run_ref[...].astype(I32)


def _ranks(idx_t, n_exp):
    n_tok = idx_t.shape[1]
    tm = 512 if n_tok % 512 == 0 else 256
    return pl.pallas_call(
        functools.partial(_rank_kernel, n_exp=n_exp),
        grid=(n_tok // tm,),
        in_specs=[pl.BlockSpec((8, tm), lambda t: (0, t))],
        out_specs=[pl.BlockSpec((8, tm), lambda t: (0, t)),
                   pl.BlockSpec((n_exp, LANES), lambda t: (0, 0))],
        out_shape=[jax.ShapeDtypeStruct((8, n_tok), I32),
                   jax.ShapeDtypeStruct((n_exp, LANES), I32)],
        scratch_shapes=[pltpu.VMEM((n_exp, LANES), F32)],
        compiler_params=_params(("arbitrary",), 32),
        name="moe_rank",
    )(idx_t)


def _dest_kernel(idx_ref, rank_ref, ps_ref, dest_ref, *, n_exp):
    tm = idx_ref.shape[1]
    e_iota = lax.broadcasted_iota(I32, (n_exp, tm), 0)
    ps = ps_ref[:, 0:1]
    rows = []
    for k in range(TOP_K):
        start = jnp.sum(jnp.where(e_iota == idx_ref[k:k + 1, :], ps, 0.0), axis=0, keepdims=True)
        rows.append(start.astype(I32) + rank_ref[k:k + 1, :])
    pad_rows = dest_ref.shape[0] - TOP_K
    dest_ref[...] = jnp.concatenate(rows + [jnp.zeros((pad_rows, tm), I32)], axis=0)


def _dests(idx_t, rank_t, pad_starts):
    n_tok = idx_t.shape[1]
    n_exp = pad_starts.shape[0]
    tm = 512 if n_tok % 512 == 0 else 256
    ps = jnp.broadcast_to(pad_starts.astype(F32)[:, None], (n_exp, LANES))
    return pl.pallas_call(
        functools.partial(_dest_kernel, n_exp=n_exp),
        grid=(n_tok // tm,),
        in_specs=[pl.BlockSpec((8, tm), lambda t: (0, t)),
                  pl.BlockSpec((8, tm), lambda t: (0, t)),
                  pl.BlockSpec((n_exp, LANES), lambda t: (0, 0))],
        out_specs=pl.BlockSpec((8, tm), lambda t: (0, t)),
        out_shape=jax.ShapeDtypeStruct((8, n_tok), I32),
        compiler_params=_params(("arbitrary",), 32),
        name="moe_dest",
    )(idx_t, rank_t, ps)


def _tile_major(dest_t, tm):
    n_tok = dest_t.shape[1]
    return dest_t[:TOP_K].reshape(TOP_K, n_tok // tm, tm).transpose(1, 0, 2).reshape(-1)


def _dispatch_kernel(cnt_ref, ps_ref, h_ref, dest_ref, xs_ref, zero_ref, sem, zsem, *, n_exp, block):
    t = pl.program_id(0)
    tm = h_ref.shape[0]

    def row_copy(r, d):
        return pltpu.make_async_copy(h_ref.at[pl.ds(r, 1)], xs_ref.at[pl.ds(d, 1)], sem)

    def issue(r8, carry):
        for u in range(SUBLANES):
            r = r8 * SUBLANES + u
            for k in range(TOP_K):
                row_copy(r, dest_ref[k * tm + r]).start(priority=k % DMA_PRIORITIES)
        return carry

    lax.fori_loop(0, tm // SUBLANES, issue, 0)

    @pl.when(t == 0)
    def _():
        zero_ref[...] = jnp.zeros_like(zero_ref)

        def zero_copy(d):
            return pltpu.make_async_copy(zero_ref.at[pl.ds(0, 1)], xs_ref.at[pl.ds(d, 1)], zsem)

        for e in range(n_exp):
            cnt = cnt_ref[e]
            first = ps_ref[e] + cnt
            n_pad = (-cnt) % block

            def zissue(i, carry, first=first):
                zero_copy(first + i).start()
                return carry

            def zwait(i, carry):
                zero_copy(0).wait()
                return carry

            lax.fori_loop(0, n_pad, zissue, 0)
            lax.fori_loop(0, n_pad, zwait, 0)

        def tail_copy(j):
            return pltpu.make_async_copy(zero_ref, xs_ref.at[pl.ds(j * block, block)], zsem)

        def tail_issue(j, carry):
            tail_copy(j).start()
            return carry

        def tail_wait(j, carry):
            tail_copy(j).wait()
            return carry

        first_tail = ps_ref[n_exp] // block
        lax.fori_loop(first_tail, xs_ref.shape[0] // block, tail_issue, 0)
        lax.fori_loop(first_tail, xs_ref.shape[0] // block, tail_wait, 0)

    for _ in range(TOP_K):
        pltpu.make_async_copy(zero_ref.at[pl.ds(0, tm)], xs_ref.at[pl.ds(0, tm)], sem).wait()


def _dispatch(h, dest_flat, counts, pad_starts, n_slots, tm):
    n_tok, d = h.shape
    n_exp = counts.shape[0]
    grid_spec = pltpu.PrefetchScalarGridSpec(
        num_scalar_prefetch=2,
        grid=(n_tok // tm,),
        in_specs=[pl.BlockSpec((tm, d), lambda t, *_: (t, 0)),
                  pl.BlockSpec((TOP_K * tm,), lambda t, *_: (t,), memory_space=pltpu.SMEM)],
        out_specs=pl.BlockSpec(memory_space=pl.ANY),
        scratch_shapes=[pltpu.VMEM((MOE_BLOCK, d), F32), pltpu.SemaphoreType.DMA, pltpu.SemaphoreType.DMA],
    )
    return pl.pallas_call(
        functools.partial(_dispatch_kernel, n_exp=n_exp, block=MOE_BLOCK),
        grid_spec=grid_spec,
        out_shape=jax.ShapeDtypeStruct((n_slots, d), F32),
        compiler_params=_params(("arbitrary",), 32),
        name="moe_dispatch",
    )(counts, pad_starts, h, dest_flat)


def _expert_kernel(be_ref, nu_ref, nv_ref, x_ref, w1_ref, b1_ref, w2_ref, b2_ref, y_ref):
    n_valid = nv_ref[pl.program_id(0)]
    half = x_ref.shape[0] // 2

    def ffn(x):
        f = w2_ref.shape[2]
        hg = jnp.dot(x.astype(BF16), w1_ref[0, 0].astype(BF16), preferred_element_type=F32) + b1_ref[0, 0]
        gate = jnp.minimum(hg[:, :f], SWIGLU_LIMIT)
        up = jnp.clip(hg[:, f:], -SWIGLU_LIMIT, SWIGLU_LIMIT)
        glu = gate * jax.nn.sigmoid(SWIGLU_ALPHA * gate)
        act = (glu * (up + 1.0)).astype(BF16)
        return jnp.dot(act, w2_ref[0, 0].astype(BF16), preferred_element_type=F32) + b2_ref[0, 0]

    @pl.when(n_valid > half)
    def _():
        y_ref[...] = ffn(x_ref[...])

    @pl.when((n_valid > 0) & (n_valid <= half))
    def _():
        y_ref[0:half] = ffn(x_ref[0:half])
        y_ref[half:] = jnp.zeros((x_ref.shape[0] - half, y_ref.shape[1]), y_ref.dtype)

    @pl.when(n_valid == 0)
    def _():
        y_ref[...] = jnp.zeros_like(y_ref)


def _experts(xs, block_expert, n_used, n_valid, w1, b1, w2, b2, layer):
    n_slots, d = xs.shape
    depth, n_exp, _, f2 = w1.shape
    f = w2.shape[2]
    bm = MOE_BLOCK

    def row_block(j, be, nu, nv):
        return (jnp.minimum(j, nu[0] - 1), 0)

    def expert_block(j, be, nu, nv):
        return (layer, be[j], 0, 0)

    grid_spec = pltpu.PrefetchScalarGridSpec(
        num_scalar_prefetch=3,
        grid=(n_slots // bm,),
        in_specs=[pl.BlockSpec((bm, d), row_block),
                  pl.BlockSpec((1, 1, d, f2), expert_block),
                  pl.BlockSpec((1, 1, 1, f2), expert_block),
                  pl.BlockSpec((1, 1, f, d), expert_block),
                  pl.BlockSpec((1, 1, 1, d), expert_block)],
        out_specs=pl.BlockSpec((bm, d), lambda j, be, nu, nv: (j, 0)),
    )
    return pl.pallas_call(
        _expert_kernel,
        grid_spec=grid_spec,
        out_shape=jax.ShapeDtypeStruct((n_slots, d), F32),
        compiler_params=_params(("arbitrary",), 62),
        name="moe_experts",
    )(block_expert, n_used, n_valid, xs, w1, b1.reshape(depth, n_exp, 1, f2), w2, b2.reshape(depth, n_exp, 1, d))


def _combine_kernel(x_ref, g_ref, dest_ref, gt_ref, fg_ref, ys_ref, o_ref, buf_ref, sem, *, final_norm):
    tm = x_ref.shape[0]

    def row_copy(k, r8, u, d):
        return pltpu.make_async_copy(ys_ref.at[pl.ds(d, 1)], buf_ref.at[k, r8, pl.ds(u, 1)], sem)

    def issue(r8, carry):
        for u in range(SUBLANES):
            for k in range(TOP_K):
                row_copy(k, r8, u, dest_ref[k * tm + r8 * SUBLANES + u]).start(priority=k % DMA_PRIORITIES)
        return carry

    lax.fori_loop(0, tm // SUBLANES, issue, 0)

    g8 = g_ref[...]
    g_cols = jnp.concatenate([g8, jnp.zeros((tm - 8, tm), F32)], axis=0).T

    for k in range(TOP_K):
        pltpu.make_async_copy(ys_ref.at[pl.ds(0, tm)], o_ref, sem).wait()

    d = x_ref.shape[1]
    acc = g_cols[:, 0:1] * buf_ref[0].reshape(tm, d)
    for k in range(1, TOP_K):
        acc = acc + g_cols[:, k:k + 1] * buf_ref[k].reshape(tm, d)
    out = x_ref[...] + gt_ref[0] * acc
    if final_norm:
        ms = jnp.mean(out * out, axis=-1, keepdims=True)
        out = out * lax.rsqrt(ms + RMS_EPS) * fg_ref[...]
    o_ref[...] = out


def _combine(xt, out_layout, gates_t, dest_flat, ys, mod3, nb, layer, final_gain, tm):
    d = xt.shape[1]
    final_norm = final_gain is not None
    fg = (final_gain if final_norm else jnp.ones((d,), F32)).reshape(1, d)
    return pl.pallas_call(
        functools.partial(_combine_kernel, final_norm=final_norm),
        grid=(out_layout.n_tok // tm,),
        in_specs=[pl.BlockSpec((tm, d), lambda t: (t, 0)),
                  pl.BlockSpec((8, tm), lambda t: (0, t)),
                  pl.BlockSpec((TOP_K * tm,), lambda t: (t,), memory_space=pltpu.SMEM),
                  _mod_spec(mod3, layer, 5, nb, out_layout, tm),
                  pl.BlockSpec((1, d), lambda t: (0, 0)),
                  pl.BlockSpec(memory_space=pl.ANY)],
        out_specs=pl.BlockSpec((tm, d), lambda t: (t, 0)),
        out_shape=jax.ShapeDtypeStruct((out_layout.n_tok, d), F32),
        scratch_shapes=[pltpu.VMEM((TOP_K, tm // SUBLANES, SUBLANES, d), F32), pltpu.SemaphoreType.DMA],
        compiler_params=_params(("arbitrary",), 32),
        name="moe_combine",
    )(xt, gates_t, dest_flat, mod3, fg, ys)


def _moe(xt, layout, out_layout, mod3, nb, layer, gain, router_w, router_b, w1, b1, w2, b2, final_gain):
    n_exp = router_w.shape[1]
    h, idx_t, gates_t = _router(xt, layout, mod3, nb, layer, gain, router_w, router_b)
    rank_t, cnt = _ranks(idx_t, n_exp)
    counts = cnt[:, 0]
    padded = (counts + MOE_BLOCK - 1) // MOE_BLOCK * MOE_BLOCK
    pad_ends = jnp.cumsum(padded)
    pad_starts = pad_ends - padded
    n_blocks = -(-layout.n_tok * TOP_K // MOE_BLOCK) + n_exp
    n_slots = n_blocks * MOE_BLOCK
    block_first_row = jnp.arange(n_blocks, dtype=I32) * MOE_BLOCK
    block_expert = jnp.minimum(jnp.sum(pad_ends[None, :] <= block_first_row[:, None], axis=1), n_exp - 1).astype(I32)
    n_used = (pad_ends[-1:] // MOE_BLOCK).astype(I32)
    dest_t = _dests(idx_t, rank_t, pad_starts)
    tm_d = layout.tile(256)
    tm_c = tm_d
    slot_bounds = jnp.concatenate([pad_starts, pad_ends[-1:]]).astype(I32)
    xs = _dispatch(h, _tile_major(dest_t, tm_d), counts, slot_bounds, n_slots, tm_d)
    rows_before = block_first_row - pad_starts[block_expert]
    n_valid = jnp.clip(counts[block_expert] - rows_before, 0, MOE_BLOCK).astype(I32)
    ys = _experts(xs, block_expert, n_used, n_valid, w1, b1, w2, b2, layer)
    return _combine(xt, out_layout, gates_t, _tile_major(dest_t, tm_c), ys, mod3, nb, layer, final_gain, tm_c)


def _nm_matmul_kernel(x_ref, gain_ref, sh_ref, sc_ref, w_ref, b_ref, o_ref, h_ref):
    @pl.when(pl.program_id(1) == 0)
    def _():
        h_ref[...] = _norm_mod(x_ref[...], gain_ref[...], sh_ref[0], sc_ref[0]).astype(BF16)

    o_ref[...] = (jnp.dot(h_ref[...], w_ref[...], preferred_element_type=F32) + b_ref[...]).astype(o_ref.dtype)


def _nm_matmul(xt, layout, mod3, nb, layer, gain, w, b, tm_want, tn):
    d = xt.shape[1]
    n = w.shape[1]
    tm = layout.tile(tm_want)
    return pl.pallas_call(
        _nm_matmul_kernel,
        grid=(layout.n_tok // tm, n // tn),
        in_specs=[pl.BlockSpec((tm, d), lambda m, j: (m, 0)),
                  pl.BlockSpec((1, d), lambda m, j: (0, 0)),
                  _mod_spec(mod3, layer, 0, nb, layout, tm),
                  _mod_spec(mod3, layer, 1, nb, layout, tm),
                  pl.BlockSpec((d, tn), lambda m, j: (0, j)),
                  pl.BlockSpec((1, tn), lambda m, j: (0, j))],
        out_specs=pl.BlockSpec((tm, tn), lambda m, j: (m, j)),
        out_shape=jax.ShapeDtypeStruct((layout.n_tok, n), BF16),
        scratch_shapes=[pltpu.VMEM((tm, d), BF16)],
        compiler_params=_params(("arbitrary", "arbitrary"), 48),
        name="norm_mod_matmul",
    )(xt, gain.reshape(1, d), mod3, mod3, w.astype(BF16), b.reshape(1, n))


def _proj_residual_kernel(a_ref, w_ref, b_ref, x_ref, gt_ref, o_ref):
    y = jnp.dot(a_ref[...], w_ref[...], preferred_element_type=F32) + b_ref[...]
    o_ref[...] = x_ref[...] + gt_ref[0] * y


def _proj_residual(a, w, b, xt, out_layout, mod3, nb, layer, tm=None, a_tile=lambda m: m):
    k, n = w.shape
    tm = out_layout.tile(512) if tm is None else tm
    assert out_layout.s_len % tm == 0
    return pl.pallas_call(
        _proj_residual_kernel,
        grid=(out_layout.n_tok // tm,),
        in_specs=[pl.BlockSpec((tm, k), lambda m: (a_tile(m), 0)),
                  pl.BlockSpec((k, n), lambda m: (0, 0)),
                  pl.BlockSpec((1, n), lambda m: (0, 0)),
                  pl.BlockSpec((tm, n), lambda m: (m, 0)),
                  _mod_spec(mod3, layer, 2, nb, out_layout, tm)],
        out_specs=pl.BlockSpec((tm, n), lambda m: (m, 0)),
        out_shape=jax.ShapeDtypeStruct((out_layout.n_tok, n), F32),
        compiler_params=_params(("arbitrary",), 48),
        name="proj_residual",
    )(a, w.astype(BF16), b.reshape(1, n), xt, mod3)


def _na_bias_slabs(rpb, rows):
    n_heads = rpb.shape[0]
    kr = min(NA_ROWS, rows)
    col = jnp.arange(GRID_W)
    c0 = jnp.clip(col - NA_COLS // 2, 0, GRID_W - NA_COLS)
    col_mask = (col[None, :] >= c0[:, None]) & (col[None, :] < c0[:, None] + NA_COLS)
    dc_idx = jnp.clip(col[None, :] - col[:, None] + NA_COLS - 1, 0, 2 * NA_COLS - 2)
    n_place = NA_ROWS
    dr = jnp.arange(n_place)[:, None] + jnp.arange(kr)[None, :]
    n_groups = n_heads // HEADS_PER_LANE_GROUP
    small = jnp.transpose(rpb[:, dr], (1, 0, 2, 3)).reshape(n_place, n_groups, HEADS_PER_LANE_GROUP, kr, -1)
    bias = small[:, :, :, jnp.arange(kr)[None, :, None], dc_idx[:, None, :]]
    bias = jnp.where(col_mask[:, None, :], bias * LOG2_E, MASK_VALUE)
    return bias.reshape(n_place, n_groups, HEADS_PER_LANE_GROUP * GRID_W, kr * GRID_W)


def _na_kernel(q_ref, k_ref, v_ref, kc_ref, vc_ref, bias_ref, o_ref, *, rows, kr, n_groups):
    r = pl.program_id(2)
    r0 = jnp.clip(r - kr // 2, 0, rows - kr)
    base = pl.multiple_of(r0 * GRID_W, GRID_W)
    n_lat, n_ctx = kr * GRID_W, kc_ref.shape[0]
    ch = NA_KEY_CHUNK
    nq = HEADS_PER_LANE_GROUP * GRID_W
    head_of_row = lax.broadcasted_iota(I32, (nq, LANES), 0) // GRID_W
    head_of_lane = lax.broadcasted_iota(I32, (nq, LANES), 1) // NA_HEAD_DIM
    own = head_of_row == head_of_lane
    for g in range(n_groups):
        gs = slice(g * LANES, (g + 1) * LANES)
        qg = q_ref[:, gs]
        qbd = jnp.where(own, jnp.concatenate([qg] * HEADS_PER_LANE_GROUP, axis=0), jnp.zeros((), BF16))
        m = l = acc = None
        for c in range((n_lat + n_ctx) // ch):
            if c < n_lat // ch:
                rows_c = pl.ds(base + c * ch, ch)
                s = _nt_dot(qbd, k_ref[rows_c, gs]) + bias_ref[0, g, :, c * ch:(c + 1) * ch]
                vals = v_ref[rows_c, gs]
            else:
                rows_c = slice(c * ch - n_lat, (c + 1) * ch - n_lat)
                s = _nt_dot(qbd, kc_ref[rows_c, gs])
                vals = vc_ref[rows_c, gs]
            m_c = jnp.max(s, axis=-1, keepdims=True)
            if m is None:
                m = m_c
                p = jnp.exp2(s - m)
                l = jnp.sum(p, axis=-1, keepdims=True)
                acc = jnp.dot(p.astype(BF16), vals, preferred_element_type=F32)
            else:
                m_new = jnp.maximum(m, m_c)
                alpha = jnp.exp2(m - m_new)
                p = jnp.exp2(s - m_new)
                l = l * alpha + jnp.sum(p, axis=-1, keepdims=True)
                acc = acc * alpha + jnp.dot(p.astype(BF16), vals, preferred_element_type=F32)
                m = m_new
        o = jnp.where(own, acc / l, 0.0)
        og = o[0:GRID_W]
        for h in range(1, HEADS_PER_LANE_GROUP):
            og = og + o[h * GRID_W:(h + 1) * GRID_W]
        o_ref[:, gs] = og.astype(o_ref.dtype)


def _neighbourhood_attention(qkv, bias_slabs, bsz, s_len, l_len, d):
    rows = s_len // GRID_W
    kr = min(NA_ROWS, rows)
    n_split = 1
    dh = d // n_split
    n_groups = dh // LANES
    ctx_block0 = bsz * s_len // l_len

    def place(r):
        r0 = jnp.clip(r - kr // 2, 0, rows - kr)
        return r0 - r + NA_ROWS - 1

    return pl.pallas_call(
        functools.partial(_na_kernel, rows=rows, kr=kr, n_groups=n_groups),
        grid=(bsz, n_split, rows),
        in_specs=[pl.BlockSpec((GRID_W, dh), lambda b, h, r: (b * rows + r, h)),
                  pl.BlockSpec((s_len, dh), lambda b, h, r: (b, n_split + h)),
                  pl.BlockSpec((s_len, dh), lambda b, h, r: (b, 2 * n_split + h)),
                  pl.BlockSpec((l_len, dh), lambda b, h, r: (ctx_block0 + b, n_split + h)),
                  pl.BlockSpec((l_len, dh), lambda b, h, r: (ctx_block0 + b, 2 * n_split + h)),
                  pl.BlockSpec((1, n_groups, HEADS_PER_LANE_GROUP * GRID_W, kr * GRID_W),
                               lambda b, h, r: (place(r), h, 0, 0))],
        out_specs=pl.BlockSpec((GRID_W, dh), lambda b, h, r: (b * rows + r, h)),
        out_shape=jax.ShapeDtypeStruct((bsz * s_len, d), BF16),
        compiler_params=_params(("arbitrary", "arbitrary", "arbitrary"), 60),
        name="neighbourhood_attention",
    )(qkv, qkv, qkv, qkv, qkv, bias_slabs)


def _dft_cos_sin(n, rows=None):
    j = jnp.arange(n, dtype=I32)
    k = j if rows is None else rows.astype(I32)
    ang = ((k[..., None] * j) % n).astype(F32) * (2.0 * jnp.pi / n)
    norm = n ** -0.5
    return jnp.cos(ang) * norm, jnp.sin(ang) * norm


def _fnet_stage1_kernel(x_ref, gain_ref, sh_ref, sc_ref, cs_ref, o_ref):
    h = _norm_mod(x_ref[...], gain_ref[...], sh_ref[0], sc_ref[0]).astype(BF16)
    dg = cs_ref.shape[0]
    for g in range(FNET_GROUPS):
        cols = slice(g * dg, (g + 1) * dg)
        a = jnp.dot(h[:, cols], cs_ref[...], preferred_element_type=F32)
        o_ref[0, 0, :, cols] = a[:, :dg].astype(o_ref.dtype)
        o_ref[0, 1, :, cols] = a[:, dg:].astype(o_ref.dtype)


def _fnet_stage1(xt, layout, mod3, nb, layer, gain):
    d = xt.shape[1]
    dg = d // FNET_GROUPS
    tm = layout.tile(512)
    tiles_per_seq = layout.s_len // tm
    cc, sc = _dft_cos_sin(dg)
    cs = jnp.concatenate([cc, sc], axis=1).astype(BF16)
    return pl.pallas_call(
        _fnet_stage1_kernel,
        grid=(layout.n_lat // tm,),
        in_specs=[pl.BlockSpec((tm, d), lambda t: (t, 0)),
                  pl.BlockSpec((1, d), lambda t: (0, 0)),
                  _mod_spec(mod3, layer, 0, nb, layout, tm),
                  _mod_spec(mod3, layer, 1, nb, layout, tm),
                  pl.BlockSpec((dg, 2 * dg), lambda t: (0, 0))],
        out_specs=pl.BlockSpec((1, 2, tm, d), lambda t: (t // tiles_per_seq, 0, t % tiles_per_seq, 0)),
        out_shape=jax.ShapeDtypeStruct((layout.bsz, 2, layout.s_len, d), BF16),
        compiler_params=_params(("arbitrary",), 40),
        name="fnet_channel_dft",
    )(xt, gain.reshape(1, d), mod3, mod3, cs)


def _fnet_stage2_kernel(lc_ref, ls_ref, rev_ref, ac_ref, as_ref, o_ref):
    tm = o_ref.shape[2]
    p = jnp.dot(lc_ref[0], ac_ref[0, 0], preferred_element_type=F32)
    q = jnp.dot(ls_ref[0], as_ref[0, 0], preferred_element_type=F32)
    o_ref[0, 0] = (p - q)[:tm].astype(o_ref.dtype)
    z = (p + q).astype(BF16)
    o_ref[0, 1] = jnp.dot(rev_ref[...], z, preferred_element_type=F32).astype(o_ref.dtype)


def _fnet_stage2(a, s_len):
    bsz, _, _, d = a.shape
    half = s_len // 2
    tm = min(512, half)
    tn = min(1024, d)
    m_tiles = half // tm
    ext = tm + SUBLANES
    rows = (jnp.arange(m_tiles)[:, None] * tm + jnp.arange(ext)[None, :]) % s_len
    lhs_cos, lhs_sin = (t.astype(BF16) for t in _dft_cos_sin(s_len, rows))
    rev = (jnp.arange(ext)[None, :] == tm - jnp.arange(tm)[:, None]).astype(BF16)
    out = pl.pallas_call(
        _fnet_stage2_kernel,
        grid=(bsz, d // tn, m_tiles),
        in_specs=[pl.BlockSpec((1, ext, s_len), lambda b, j, m: (m, 0, 0)),
                  pl.BlockSpec((1, ext, s_len), lambda b, j, m: (m, 0, 0)),
                  pl.BlockSpec((tm, ext), lambda b, j, m: (0, 0)),
                  pl.BlockSpec((1, 1, s_len, tn), lambda b, j, m: (b, 0, 0, j)),
                  pl.BlockSpec((1, 1, s_len, tn), lambda b, j, m: (b, 1, 0, j))],
        out_specs=pl.BlockSpec((1, 2, tm, tn), lambda b, j, m: (b, 0, m, j)),
        out_shape=jax.ShapeDtypeStruct((bsz, 2, half, d), BF16),
        compiler_params=_params(("arbitrary", "arbitrary", "arbitrary"), 48),
        name="fnet_position_dft",
    )(lhs_cos, lhs_sin, rev, a, a)

    def stored_tile(t):
        b, w = t // (2 * m_tiles), t % (2 * m_tiles)
        return b * 2 * m_tiles + jnp.where(w < m_tiles, w, 3 * m_tiles - 1 - w)

    return out.reshape(bsz * s_len, d), tm, stored_tile


def kernel(x, c, ctx, c_ctx, w_mod, b_mod, norm_gain, final_gain, pool_w, pool_scale, na_w_qkv, na_b_qkv, na_rpb, na_w_o, na_b_o, fnet_w, fnet_b, router_w, router_b, exp_w1, exp_b1, exp_w2, exp_b2):
    bsz, s_len, d = x.shape
    l_len = ctx.shape[1]
    depth = w_mod.shape[0]
    last_ctx_layer = ((depth - 2) // N_MIXERS) * N_MIXERS + 1

    nb = -(-(bsz + 1) // 8) * 8
    c_all = jnp.concatenate([c, c_ctx[None], jnp.zeros((nb - bsz - 1, d), F32)], axis=0)
    mod = _modulation(c_all, w_mod, b_mod)
    mod3 = mod.reshape(depth, nb, N_MOD, d).transpose(0, 2, 1, 3).reshape(depth * N_MOD * nb, 1, d)

    lat_only = _Layout(bsz, s_len, l_len, False)
    with_ctx = _Layout(bsz, s_len, l_len, True)
    x_lat, x_ctx = x.reshape(bsz * s_len, d), ctx.reshape(bsz * l_len, d)
    xt = None
    for i in range(depth):
        kind, j = i % N_MIXERS, i // N_MIXERS
        ctx_in = i <= last_ctx_layer
        ctx_out = i < last_ctx_layer
        layout_in = with_ctx if ctx_in else lat_only
        layout_out = with_ctx if ctx_out else lat_only
        if kind == 0:
            if xt is None:
                srcs = (x_lat, x_ctx if ctx_out else None)
            else:
                srcs = (xt, None)
            xt = _pool_mixer(*srcs, layout_out, mod3, nb, i, norm_gain[i, 0], pool_w[j], pool_scale[j])
        elif kind == 1:
            if ctx_in:
                k_fold = jnp.concatenate([jnp.ones((d,), F32), jnp.full((d,), NA_HEAD_DIM ** -0.5 * LOG2_E, F32),
                                          jnp.ones((d,), F32)])
                qkv = _nm_matmul(xt, layout_in, mod3, nb, i, norm_gain[i, 0], na_w_qkv[j] * k_fold,
                                 na_b_qkv[j] * k_fold, 1024, min(1024, d))
                slabs = _na_bias_slabs(na_rpb[j], s_len // GRID_W)
                att = _neighbourhood_attention(qkv, slabs, bsz, s_len, l_len, d)
                xt = _proj_residual(att, na_w_o[j], na_b_o[j], xt, layout_out, mod3, nb, i)
            else:
                raise NotImplementedError("neighbourhood attention without a live context stream")
        else:
            a = _fnet_stage1(xt, layout_out, mod3, nb, i, norm_gain[i, 0])
            f, f_tile, stored_tile = _fnet_stage2(a, s_len)
            xt = _proj_residual(f, fnet_w[j], fnet_b[j], xt, layout_out, mod3, nb, i, tm=f_tile, a_tile=stored_tile)
        xt = _moe(xt, layout_out, layout_out, mod3, nb, i, norm_gain[i, 1], router_w[i], router_b[i],
                  exp_w1, exp_b1, exp_w2, exp_b2, final_gain if i == depth - 1 else None)
    return xt[:bsz * s_len].reshape(bsz, s_len, d)
```
